```python
import math
import jax, jax.numpy as jnp
from jax import lax
import numpy as np

D_MODEL = 4096
BATCH = 4
SEQ = 4096
DEPTH = 1

CTX_LEN = 256
GRID_W = 64
N_HEADS = 16
N_KV_HEADS = 4
HEAD_DIM = 128
ATTN_WIDTH = N_HEADS * HEAD_DIM
KV_WIDTH = N_KV_HEADS * HEAD_DIM
ROPE_THETA = 10000.0
Q_BLOCK = 128
SGU_CHUNK = 128
SGU_GROUPS = 8
SGU_WIDTH = D_MODEL // 2
SGU_GROUP_DIM = SGU_WIDTH // SGU_GROUPS
N_EXPERTS = 16
EC_CAPACITY_FACTOR = 2
D_FF_EXPERT = D_MODEL // 2
NORM_EPS = 1e-6
IN_WIDTH = ATTN_WIDTH + 2 * KV_WIDTH + 2 * SGU_WIDTH + 2 * D_MODEL
SPLIT_POINTS = (
    ATTN_WIDTH,
    ATTN_WIDTH + KV_WIDTH,
    ATTN_WIDTH + 2 * KV_WIDTH,
    ATTN_WIDTH + 2 * KV_WIDTH + SGU_WIDTH,
    ATTN_WIDTH + 2 * KV_WIDTH + 2 * SGU_WIDTH,
    ATTN_WIDTH + 2 * KV_WIDTH + 2 * SGU_WIDTH + D_MODEL,
)

kernel_name = "hybrid_gqa_sgu_ec_moe_dit_block"


def rms_norm(x, gain):
    xf = x.astype(jnp.float32)
    y = xf * lax.rsqrt(jnp.mean(xf * xf, axis=-1, keepdims=True) + NORM_EPS)
    return (y * gain.astype(jnp.float32)).astype(x.dtype)


def modulate(h, shift, scale):
    return h * (1 + scale) + shift


def axial_rope_tables(n_tokens):
    rows = n_tokens // GRID_W
    pos_row = jnp.repeat(jnp.arange(rows, dtype=jnp.float32), GRID_W)
    pos_col = jnp.tile(jnp.arange(GRID_W, dtype=jnp.float32), rows)
    n_freq = HEAD_DIM // 4
    inv_freq = ROPE_THETA ** (-jnp.arange(n_freq, dtype=jnp.float32) / n_freq)
    ang = jnp.stack([pos_row[:, None] * inv_freq, pos_col[:, None] * inv_freq], axis=1)
    return jnp.cos(ang), jnp.sin(ang)


def apply_axial_rope(x, cos, sin):
    b, n, h, d = x.shape
    xf = x.astype(jnp.float32).reshape(b, n, h, 2, 2, d // 4)
    x1, x2 = xf[..., 0, :], xf[..., 1, :]
    c = cos[None, :, None]
    s = sin[None, :, None]
    out = jnp.stack([x1 * c - x2 * s, x2 * c + x1 * s], axis=-2)
    return out.reshape(b, n, h, d).astype(x.dtype)


def split_projection(p, q_norm, k_norm):
    b, n, _ = p.shape
    q, k, v, u, s, ga, gb = jnp.split(p, SPLIT_POINTS, axis=-1)
    q = rms_norm(q.reshape(b, n, N_HEADS, HEAD_DIM), q_norm)
    k = rms_norm(k.reshape(b, n, N_KV_HEADS, HEAD_DIM), k_norm)
    v = v.reshape(b, n, N_KV_HEADS, HEAD_DIM)
    return q, k, v, jax.nn.gelu(u), jax.nn.gelu(s), ga, gb


def context_kv(a_ctx, w_in, k_norm):
    b, n, _ = a_ctx.shape
    kv = a_ctx @ w_in[:, SPLIT_POINTS[0]:SPLIT_POINTS[2]]
    k, v = jnp.split(kv, 2, axis=-1)
    k = rms_norm(k.reshape(b, n, N_KV_HEADS, HEAD_DIM), k_norm)
    return k, v.reshape(b, n, N_KV_HEADS, HEAD_DIM)


def attend_blocks(q, k, v):
    b, nq, h, dh = q.shape
    g = h // N_KV_HEADS
    nblk = nq // Q_BLOCK
    qb = q.reshape(b, nblk, Q_BLOCK, N_KV_HEADS, g, dh).transpose(1, 0, 2, 3, 4, 5)
    scale = dh ** -0.5

    def one_block(qblk):
        s = jnp.einsum('bqkgd,bskd->bkgqs', qblk, k, preferred_element_type=jnp.float32) * scale
        p = jax.nn.softmax(s, axis=-1).astype(v.dtype)
        return jnp.einsum('bkgqs,bskd->bqkgd', p, v)

    o = lax.map(one_block, qb)
    return o.transpose(1, 0, 2, 3, 4, 5).reshape(b, nq, h * dh)


def spatial_gating(u, v, sgu_norm, sgu_w, sgu_b):
    b, n, _ = v.shape
    vg = v.reshape(b, n // SGU_CHUNK, SGU_CHUNK, SGU_GROUPS, SGU_GROUP_DIM)
    vg = rms_norm(vg, sgu_norm.reshape(SGU_GROUPS, SGU_GROUP_DIM))
    s = jnp.einsum('gpq,bcqgd->bcpgd', sgu_w, vg) + sgu_b.T[None, None, :, :, None]
    return u * s.reshape(b, n, SGU_WIDTH)


def mix_and_merge(q, k, v, u, s, ga, gb, sgu_norm, sgu_w, sgu_b, w_br_attn, w_br_sgu, w_out):
    attn = attend_blocks(q, k, v)
    sgu = spatial_gating(u, s, sgu_norm, sgu_w, sgu_b)
    merged = jax.nn.sigmoid(ga) * (attn @ w_br_attn) + jax.nn.sigmoid(gb) * (sgu @ w_br_sgu)
    return merged @ w_out


def expert_choice_ffn(h, w_router, w_gate, w_up, w_down):
    b, n, d = h.shape
    cap = EC_CAPACITY_FACTOR * n // N_EXPERTS
    aff = jax.nn.softmax(jnp.einsum('bnd,de->bne', h, w_router, preferred_element_type=jnp.float32), axis=-1)
    gsel, idx = lax.top_k(aff.transpose(0, 2, 1), cap)
    xin = jax.vmap(lambda hs, ids: hs[ids])(h, idx)
    hid = jax.nn.silu(jnp.einsum('becd,edf->becf', xin, w_gate)) * jnp.einsum('becd,edf->becf', xin, w_up)
    y = jnp.einsum('becf,efd->becd', hid, w_down) * gsel[..., None].astype(h.dtype)
    return jax.vmap(lambda ids, vals: jnp.zeros((n, d), vals.dtype).at[ids.reshape(-1)].add(vals.reshape(-1, d)))(idx, y)


def setup_inputs(seed: int = 0) -> dict:
    key = jax.random.key(seed)
    ks = jax.random.split(key, 24)
    f32 = jnp.float32
    D, L = D_MODEL, DEPTH

    def nrm(k, shape, scale):
        return jax.random.normal(k, shape, f32) * scale

    def gain(k, shape):
        return 1.0 + 0.05 * jax.random.normal(k, shape, f32)

    return {
        "x": nrm(ks[0], (BATCH, SEQ, D), 1.0),
        "c": nrm(ks[1], (BATCH, D), 1.0),
        "ctx": nrm(ks[2], (BATCH, CTX_LEN, D), 1.0),
        "c_ctx": nrm(ks[3], (D,), 1.0),
        "w_mod": nrm(ks[4], (L, D, 6 * D), 0.5 * D ** -0.5),
        "b_mod": nrm(ks[5], (L, 6 * D), 0.02),
        "pre_mix_norm": gain(ks[6], (L, D)),
        "post_mix_norm": gain(ks[7], (L, D)),
        "pre_ffn_norm": gain(ks[8], (L, D)),
        "post_ffn_norm": gain(ks[9], (L, D)),
        "w_in": nrm(ks[10], (L, D, IN_WIDTH), D ** -0.5),
        "q_norm": gain(ks[11], (L, HEAD_DIM)),
        "k_norm": gain(ks[12], (L, HEAD_DIM)),
        "sgu_norm": gain(ks[13], (L, SGU_WIDTH)),
        "sgu_w": nrm(ks[14], (L, SGU_GROUPS, SGU_CHUNK, SGU_CHUNK), SGU_CHUNK ** -0.5),
        "sgu_b": gain(ks[15], (L, SGU_GROUPS, SGU_CHUNK)),
        "w_br_attn": nrm(ks[16], (L, ATTN_WIDTH, D), ATTN_WIDTH ** -0.5),
        "w_br_sgu": nrm(ks[17], (L, SGU_WIDTH, D), SGU_WIDTH ** -0.5),
        "w_out": nrm(ks[18], (L, D, D), D ** -0.5),
        "w_router": nrm(ks[19], (L, D, N_EXPERTS), D ** -0.5),
        "w_gate": nrm(ks[20], (L, N_EXPERTS, D, D_FF_EXPERT), D ** -0.5),
        "w_up": nrm(ks[21], (L, N_EXPERTS, D, D_FF_EXPERT), D ** -0.5),
        "w_down": nrm(ks[22], (L, N_EXPERTS, D_FF_EXPERT, D), D_FF_EXPERT ** -0.5),
    }


def reference(x, c, ctx, c_ctx, w_mod, b_mod, pre_mix_norm, post_mix_norm, pre_ffn_norm, post_ffn_norm,
              w_in, q_norm, k_norm, sgu_norm, sgu_w, sgu_b, w_br_attn, w_br_sgu, w_out,
              w_router, w_gate, w_up, w_down):
    n = x.shape[1]
    cos, sin = axial_rope_tables(n)
    silu_c = jax.nn.silu(c)
    silu_cc = jax.nn.silu(c_ctx)
    h_lat, h_ctx = x, ctx
    for l in range(DEPTH):
        last = l == DEPTH - 1
        mod_lat = (silu_c @ w_mod[l] + b_mod[l])[:, None, :]
        mod_ctx = (silu_cc @ w_mod[l] + b_mod[l])[None, None, :]
        sh1, sc1, g1, sh2, sc2, g2 = jnp.split(mod_lat, 6, axis=-1)
        csh1, csc1, cg1, csh2, csc2, cg2 = jnp.split(mod_ctx, 6, axis=-1)
        mixer_w = (sgu_norm[l], sgu_w[l], sgu_b[l], w_br_attn[l], w_br_sgu[l], w_out[l])

        a_lat = modulate(rms_norm(h_lat, pre_mix_norm[l]), sh1, sc1)
        a_ctx = modulate(rms_norm(h_ctx, pre_mix_norm[l]), csh1, csc1)
        q_l, k_l, v_l, u_l, s_l, ga_l, gb_l = split_projection(a_lat @ w_in[l], q_norm[l], k_norm[l])
        q_l = apply_axial_rope(q_l, cos, sin)
        k_l = apply_axial_rope(k_l, cos, sin)
        if last:
            k_c, v_c = context_kv(a_ctx, w_in[l], k_norm[l])
        else:
            q_c, k_c, v_c, u_c, s_c, ga_c, gb_c = split_projection(a_ctx @ w_in[l], q_norm[l], k_norm[l])
            y_c = mix_and_merge(q_c, k_c, v_c, u_c, s_c, ga_c, gb_c, *mixer_w)
        k_all = jnp.concatenate([k_l, k_c], axis=1)
        v_all = jnp.concatenate([v_l, v_c], axis=1)
        y_l = mix_and_merge(q_l, k_all, v_all, u_l, s_l, ga_l, gb_l, *mixer_w)
        h_lat = h_lat + g1 * rms_norm(y_l, post_mix_norm[l])

        b_lat = modulate(rms_norm(h_lat, pre_ffn_norm[l]), sh2, sc2)
        f_lat = expert_choice_ffn(b_lat, w_router[l], w_gate[l], w_up[l], w_down[l])
        h_lat = h_lat + g2 * rms_norm(f_lat, post_ffn_norm[l])
        if not last:
            h_ctx = h_ctx + cg1 * rms_norm(y_c, post_mix_norm[l])
            b_ctx = modulate(rms_norm(h_ctx, pre_ffn_norm[l]), csh2, csc2)
            f_ctx = expert_choice_ffn(b_ctx, w_router[l], w_gate[l], w_up[l], w_down[l])
            h_ctx = h_ctx + cg2 * rms_norm(f_ctx, post_ffn_norm[l])
    return h_lat
```

```python
import functools
import math

import jax
import jax.numpy as jnp
from jax import lax
from jax.experimental import pallas as pl
from jax.experimental.pallas import tpu as pltpu

F32 = jnp.float32
BF16 = jnp.bfloat16

GRID_W = 64
N_HEADS = 16
N_KV_HEADS = 4
HEAD_DIM = 128
GQA_GROUP = N_HEADS // N_KV_HEADS
ATTN_WIDTH = N_HEADS * HEAD_DIM
KV_WIDTH = N_KV_HEADS * HEAD_DIM
ROPE_THETA = 10000.0
SGU_CHUNK = 128
SGU_GROUPS = 8
EC_CAPACITY_FACTOR = 2
NORM_EPS = 1e-6
N_MOD = 6

LANES = 128
SUBLANES = 8
VMEM_BUDGET_BYTES = 56 * 1024 * 1024


def _cparams(n_axes, vmem=VMEM_BUDGET_BYTES):
    return pltpu.CompilerParams(
        dimension_semantics=("arbitrary",) * n_axes, vmem_limit_bytes=vmem)


def _rms(x, gain):
    return x * lax.rsqrt(jnp.mean(x * x, axis=-1, keepdims=True) + NORM_EPS) * gain


def _dot(a, b):
    return jnp.dot(a, b, preferred_element_type=F32)


def _dot_nt(a, b):
    return lax.dot_general(a, b, (((1,), (1,)), ((), ())), preferred_element_type=F32)


def _dot_tn(a, b):
    return lax.dot_general(a, b, (((0,), (0,)), ((), ())), preferred_element_type=F32)


def _mod_kernel(c_ref, w_ref, b_ref, o_ref):
    c = c_ref[...]
    s = (c * jax.nn.sigmoid(c)).astype(BF16)
    o_ref[...] = _dot(s, w_ref[...].astype(BF16)) + b_ref[...]


def _mod_vectors(cc, w_mod, b_mod):
    rows, d = cc.shape
    n = w_mod.shape[1]
    tn = min(n, 512)
    return pl.pallas_call(
        _mod_kernel,
        grid=(n // tn,),
        in_specs=[pl.BlockSpec((rows, d), lambda j: (0, 0)),
                  pl.BlockSpec((d, tn), lambda j: (0, j)),
                  pl.BlockSpec((1, tn), lambda j: (0, j))],
        out_specs=pl.BlockSpec((rows, tn), lambda j: (0, j)),
        out_shape=jax.ShapeDtypeStruct((rows, n), F32),
        compiler_params=_cparams(1),
        name="mod_vectors",
    )(cc, w_mod, b_mod)


def _norm_mod_kernel(x_ref, g_ref, sh_ref, sc_ref, o_ref):
    x = x_ref[0]
    o_ref[0] = (_rms(x, g_ref[...]) * (1.0 + sc_ref[0]) + sh_ref[0]).astype(BF16)


def _norm_mod(x, gain, shift, scale, row_of_batch):
    b, n, d = x.shape
    tm = min(n, 512)
    return pl.pallas_call(
        _norm_mod_kernel,
        grid=(b, n // tm),
        in_specs=[pl.BlockSpec((1, tm, d), lambda bi, i: (bi, i, 0)),
                  pl.BlockSpec((1, d), lambda bi, i: (0, 0)),
                  pl.BlockSpec((1, 1, d), lambda bi, i: (row_of_batch(bi), 0, 0)),
                  pl.BlockSpec((1, 1, d), lambda bi, i: (row_of_batch(bi), 0, 0))],
        out_specs=pl.BlockSpec((1, tm, d), lambda bi, i: (bi, i, 0)),
        out_shape=jax.ShapeDtypeStruct((b, n, d), BF16),
        compiler_params=_cparams(2),
        name="norm_modulate",
    )(x, gain, shift, scale)


def _head_norm_rope(acc, gain, cos, sin, lane_lo):
    outs = []
    for h in range(acc.shape[1] // HEAD_DIM):
        y = _rms(acc[:, h * HEAD_DIM:(h + 1) * HEAD_DIM], gain)
        if cos is not None:
            partner = jnp.where(lane_lo, pltpu.roll(y, HEAD_DIM - HEAD_DIM // 4, axis=1),
                                pltpu.roll(y, HEAD_DIM // 4, axis=1))
            y = y * cos + partner * sin
        outs.append(y)
    return outs[0] if len(outs) == 1 else jnp.concatenate(outs, axis=1)


def _in_proj_kernel(bounds, group_dim, a_ref, w_ref, cos_ref, sin_ref, qn_ref, kn_ref, sn_ref,
                    q_ref, k_ref, v_ref, u_ref, s_ref, ga_ref, gb_ref):
    j = pl.program_id(1)
    acc = _dot(a_ref[...], w_ref[...])
    bq, bk, bv, bu, bs, bga = bounds
    lane = lax.broadcasted_iota(jnp.int32, (1, HEAD_DIM), 1)
    lane_lo = (lane % (HEAD_DIM // 2)) < (HEAD_DIM // 4)

    @pl.when(j < bq)
    def _():
        q_ref[...] = _head_norm_rope(acc, qn_ref[...], cos_ref[...], sin_ref[...], lane_lo).astype(BF16)

    @pl.when((j >= bq) & (j < bk))
    def _():
        k_ref[...] = _head_norm_rope(acc, kn_ref[...], cos_ref[...], sin_ref[...], lane_lo).astype(BF16)

    @pl.when((j >= bk) & (j < bv))
    def _():
        v_ref[...] = acc.astype(BF16)

    @pl.when((j >= bv) & (j < bu))
    def _():
        u_ref[...] = jax.nn.gelu(acc).astype(BF16)

    @pl.when((j >= bu) & (j < bs))
    def _():
        g = jax.nn.gelu(acc)
        gain = sn_ref[...]
        parts = [_rms(g[:, c * group_dim:(c + 1) * group_dim], gain[:, c * group_dim:(c + 1) * group_dim])
                 for c in range(acc.shape[1] // group_dim)]
        s_ref[...] = (parts[0] if len(parts) == 1 else jnp.concatenate(parts, axis=1)).astype(BF16)

    @pl.when((j >= bs) & (j < bga))
    def _():
        ga_ref[...] = jax.nn.sigmoid(acc).astype(BF16)

    @pl.when(j >= bga)
    def _():
        gb_ref[...] = jax.nn.sigmoid(acc).astype(BF16)


def _in_proj(a, w, cos, sin, q_norm, k_norm, sgu_norm, seq, d):
    m = a.shape[0]
    sw = d // 2
    gd = sw // SGU_GROUPS
    tm = min(seq, 512)
    tn = 512
    widths = (ATTN_WIDTH, KV_WIDTH, KV_WIDTH, sw, sw, d, d)
    starts = [0]
    for wd in widths:
        assert wd % tn == 0
        starts.append(starts[-1] + wd // tn)
    bounds = tuple(starts[1:7])
    pos_tiles = seq // tm

    def region(r):
        lo, hi = starts[r], starts[r + 1]
        return lambda i, j: (i, jnp.clip(j - lo, 0, hi - lo - 1))

    s_lo, s_n = starts[4], starts[5] - starts[4]
    out_shapes = [jax.ShapeDtypeStruct((m, wd), BF16) for wd in widths]
    return pl.pallas_call(
        functools.partial(_in_proj_kernel, bounds, gd),
        grid=(m // tm, starts[-1]),
        in_specs=[pl.BlockSpec((tm, d), lambda i, j: (i, 0)),
                  pl.BlockSpec((d, tn), lambda i, j: (0, j)),
                  pl.BlockSpec((tm, HEAD_DIM), lambda i, j: (i % pos_tiles, 0)),
                  pl.BlockSpec((tm, HEAD_DIM), lambda i, j: (i % pos_tiles, 0)),
                  pl.BlockSpec((1, HEAD_DIM), lambda i, j: (0, 0)),
                  pl.BlockSpec((1, HEAD_DIM), lambda i, j: (0, 0)),
                  pl.BlockSpec((1, tn), lambda i, j: (0, jnp.clip(j - s_lo, 0, s_n - 1)))],
        out_specs=[pl.BlockSpec((tm, tn), region(r)) for r in range(7)],
        out_shape=out_shapes,
        compiler_params=_cparams(2),
        name="in_projection",
    )(a, w, cos, sin, q_norm, k_norm, sgu_norm)


def _ctx_kv_kernel(a_ref, wk_ref, wv_ref, kn_ref, k_ref, v_ref):
    a = a_ref[...]
    k_ref[...] = _head_norm_rope(_dot(a, wk_ref[...]), kn_ref[...], None, None, None).astype(BF16)
    v_ref[...] = _dot(a, wv_ref[...]).astype(BF16)


def _ctx_kv(a, w, k_norm, d):
    m = a.shape[0]
    tm = min(m, 512)
    kb = ATTN_WIDTH // KV_WIDTH
    return pl.pallas_call(
        _ctx_kv_kernel,
        grid=(m // tm,),
        in_specs=[pl.BlockSpec((tm, d), lambda i: (i, 0)),
                  pl.BlockSpec((d, KV_WIDTH), lambda i: (0, kb)),
                  pl.BlockSpec((d, KV_WIDTH), lambda i: (0, kb + 1)),
                  pl.BlockSpec((1, HEAD_DIM), lambda i: (0, 0))],
        out_specs=[pl.BlockSpec((tm, KV_WIDTH), lambda i: (i, 0))] * 2,
        out_shape=[jax.ShapeDtypeStruct((m, KV_WIDTH), BF16)] * 2,
        compiler_params=_cparams(1),
        name="context_kv",
    )(a, w, w, k_norm)


def _attn_kernel(q_ref, kl_ref, vl_ref, kc_ref, vc_ref, o_ref):
    tq = q_ref.shape[1]
    q = q_ref[0]
    qs = jnp.concatenate([q[:, g * HEAD_DIM:(g + 1) * HEAD_DIM] for g in range(GQA_GROUP)], axis=0)
    scale = HEAD_DIM ** -0.5
    s_l = _dot_nt(qs, kl_ref[0]) * scale
    s_c = _dot_nt(qs, kc_ref[0]) * scale
    m = jnp.maximum(jnp.max(s_l, axis=1, keepdims=True), jnp.max(s_c, axis=1, keepdims=True))
    p_l = jnp.exp(s_l - m)
    p_c = jnp.exp(s_c - m)
    denom = jnp.sum(p_l, axis=1, keepdims=True) + jnp.sum(p_c, axis=1, keepdims=True)
    o = (_dot(p_l.astype(BF16), vl_ref[0]) + _dot(p_c.astype(BF16), vc_ref[0])) / denom
    for g in range(GQA_GROUP):
        o_ref[0, :, g * HEAD_DIM:(g + 1) * HEAD_DIM] = o[g * tq:(g + 1) * tq].astype(BF16)


def _attention(q, k_lat, v_lat, k_ctx, v_ctx):
    b, n, _ = q.shape
    n_ctx = k_ctx.shape[1]
    tq = min(n, 128)
    qw = GQA_GROUP * HEAD_DIM
    return pl.pallas_call(
        _attn_kernel,
        grid=(b, N_KV_HEADS, n // tq),
        in_specs=[pl.BlockSpec((1, tq, qw), lambda bi, h, i: (bi, i, h)),
                  pl.BlockSpec((1, n, HEAD_DIM), lambda bi, h, i: (bi, 0, h)),
                  pl.BlockSpec((1, n, HEAD_DIM), lambda bi, h, i: (bi, 0, h)),
                  pl.BlockSpec((1, n_ctx, HEAD_DIM), lambda bi, h, i: (bi, 0, h)),
                  pl.BlockSpec((1, n_ctx, HEAD_DIM), lambda bi, h, i: (bi, 0, h))],
        out_specs=pl.BlockSpec((1, tq, qw), lambda bi, h, i: (bi, i, h)),
        out_shape=jax.ShapeDtypeStruct((b, n, ATTN_WIDTH), BF16),
        compiler_params=_cparams(3),
        name="gqa_attention",
    )(q, k_lat, v_lat, k_ctx, v_ctx)


def _merge_kernel(group_dim, attn_ref, u_ref, s_ref, sw_ref, sbt_ref, wa_ref, ws_ref, ga_ref, gb_ref,
                  o_ref, sgu_scr):
    j = pl.program_id(1)
    tm = attn_ref.shape[0]

    @pl.when(j == 0)
    def _():
        for g in range(SGU_GROUPS):
            wg = sw_ref[g].astype(BF16)
            bias = sbt_ref[:, g:g + 1]
            cols = slice(g * group_dim, (g + 1) * group_dim)
            for c in range(tm // SGU_CHUNK):
                rows = slice(c * SGU_CHUNK, (c + 1) * SGU_CHUNK)
                mixed = _dot(wg, s_ref[rows, cols]) + bias
                sgu_scr[rows, cols] = (u_ref[rows, cols].astype(F32) * mixed).astype(BF16)

    ya = _dot(attn_ref[...], wa_ref[...])
    ys = _dot(sgu_scr[...], ws_ref[...])
    o_ref[...] = (ga_ref[...].astype(F32) * ya + gb_ref[...].astype(F32) * ys).astype(BF16)


def _merge(attn, u, s, sgu_w, sgu_bt, w_br_attn, w_br_sgu, ga, gb, seq, d):
    m = attn.shape[0]
    sw = d // 2
    tm = min(seq, 512)
    tn = 512
    return pl.pallas_call(
        functools.partial(_merge_kernel, sw // SGU_GROUPS),
        grid=(m // tm, d // tn),
        in_specs=[pl.BlockSpec((tm, ATTN_WIDTH), lambda i, j: (i, 0)),
                  pl.BlockSpec((tm, sw), lambda i, j: (i, 0)),
                  pl.BlockSpec((tm, sw), lambda i, j: (i, 0)),
                  pl.BlockSpec((SGU_GROUPS, SGU_CHUNK, SGU_CHUNK), lambda i, j: (0, 0, 0)),
                  pl.BlockSpec((SGU_CHUNK, SGU_GROUPS), lambda i, j: (0, 0)),
                  pl.BlockSpec((ATTN_WIDTH, tn), lambda i, j: (0, j)),
                  pl.BlockSpec((sw, tn), lambda i, j: (0, j)),
                  pl.BlockSpec((tm, tn), lambda i, j: (i, j)),
                  pl.BlockSpec((tm, tn), lambda i, j: (i, j))],
        out_specs=pl.BlockSpec((tm, tn), lambda i, j: (i, j)),
        out_shape=jax.ShapeDtypeStruct((m, d), BF16),
        scratch_shapes=[pltpu.VMEM((tm, sw), BF16)],
        compiler_params=_cparams(2),
        name="sgu_branch_merge",
    )(attn, u, s, sgu_w, sgu_bt, w_br_attn, w_br_sgu, ga, gb)


def _out_proj_kernel(m_ref, w_ref, x_ref, pmn_ref, g1_ref, pfn_ref, sh2_ref, sc2_ref,
                     h_ref, b_ref, y_scr):
    j = pl.program_id(1)
    y_scr[j] = _dot(m_ref[...], w_ref[...])

    @pl.when(j == pl.num_programs(1) - 1)
    def _():
        y = jnp.concatenate([y_scr[t] for t in range(y_scr.shape[0])], axis=1)
        h = x_ref[...] + g1_ref[0] * _rms(y, pmn_ref[...])
        h_ref[...] = h
        b_ref[...] = (_rms(h, pfn_ref[...]) * (1.0 + sc2_ref[0]) + sh2_ref[0]).astype(BF16)


def _out_proj(merged, w_out, x, post_mix, g1, pre_ffn, sh2, sc2, seq, d):
    m = merged.shape[0]
    tm = min(seq, 256)
    tn = 512
    tiles_per_batch = seq // tm
    row = lambda i, j: (i // tiles_per_batch, 0, 0)
    return pl.pallas_call(
        _out_proj_kernel,
        grid=(m // tm, d // tn),
        in_specs=[pl.BlockSpec((tm, d), lambda i, j: (i, 0)),
                  pl.BlockSpec((d, tn), lambda i, j: (0, j)),
                  pl.BlockSpec((tm, d), lambda i, j: (i, 0)),
                  pl.BlockSpec((1, d), lambda i, j: (0, 0)),
                  pl.BlockSpec((1, 1, d), row),
                  pl.BlockSpec((1, d), lambda i, j: (0, 0)),
                  pl.BlockSpec((1, 1, d), row),
                  pl.BlockSpec((1, 1, d), row)],
        out_specs=[pl.BlockSpec((tm, d), lambda i, j: (i, 0)),
                   pl.BlockSpec((tm, d), lambda i, j: (i, 0))],
        out_shape=[jax.ShapeDtypeStruct((m, d), F32), jax.ShapeDtypeStruct((m, d), BF16)],
        scratch_shapes=[pltpu.VMEM((d // tn, tm, tn), F32)],
        compiler_params=_cparams(2),
        name="out_projection",
    )(merged, w_out, x, post_mix, g1, pre_ffn, sh2, sc2)


def _split_bf16(x):
    hi = x.astype(BF16)
    return hi, (x - hi.astype(F32)).astype(BF16)


def _router_kernel(h_ref, pfn_ref, sh2_ref, sc2_ref, wr_ref, aff_ref):
    n_e = wr_ref.shape[0]
    bl = _rms(h_ref[0], pfn_ref[...]) * (1.0 + sc2_ref[0]) + sh2_ref[0]
    b_hi, b_lo = _split_bf16(bl)
    w_hi, w_lo = _split_bf16(wr_ref[...])
    p1 = _dot_nt(jnp.concatenate([w_hi, w_lo], axis=0), b_hi)
    logits = p1[:n_e] + p1[n_e:] + _dot_nt(w_hi, b_lo)
    z = jnp.exp(logits - jnp.max(logits, axis=0, keepdims=True))
    aff_ref[0] = z / jnp.sum(z, axis=0, keepdims=True)


def _router(h, pre_ffn, sh2, sc2, w_router_t):
    b, n, d = h.shape
    n_e = w_router_t.shape[0]
    t = min(n, 512)
    row = lambda bi, i: (bi, 0, 0)
    return pl.pallas_call(
        _router_kernel,
        grid=(b, n // t),
        in_specs=[pl.BlockSpec((1, t, d), lambda bi, i: (bi, i, 0)),
                  pl.BlockSpec((1, d), lambda bi, i: (0, 0)),
                  pl.BlockSpec((1, 1, d), row),
                  pl.BlockSpec((1, 1, d), row),
                  pl.BlockSpec((n_e, d), lambda bi, i: (0, 0))],
        out_specs=pl.BlockSpec((1, n_e, t), lambda bi, i: (bi, 0, i)),
        out_shape=jax.ShapeDtypeStruct((b, n_e, n), F32),
        compiler_params=_cparams(2),
        name="router_affinity",
    )(h, pre_ffn, sh2, sc2, w_router_t)


def _prefix_count(x01, tri):
    rows, n = x01.shape
    off = jnp.zeros((rows, 1), F32)
    outs = []
    for c in range(n // LANES):
        y = _dot(x01[:, c * LANES:(c + 1) * LANES].astype(BF16), tri) + off
        outs.append(y)
        off = y[:, LANES - 1:LANES]
    return jnp.concatenate(outs, axis=1)


def _select_kernel(cap, aff_ref, slot_ref, gate_ref):
    aff = aff_ref[0]
    n_e, n = aff.shape
    bits = pltpu.bitcast(aff, jnp.int32)
    r = lax.broadcasted_iota(jnp.int32, (LANES, LANES), 0)
    c = lax.broadcasted_iota(jnp.int32, (LANES, LANES), 1)
    tri = (r <= c).astype(BF16)

    def body(i, ans):
        cand = ans | lax.shift_left(jnp.int32(1), 30 - i)
        cnt = jnp.sum((bits >= cand).astype(F32), axis=1, keepdims=True)
        return jnp.where(cnt >= cap, cand, ans)

    kth = lax.fori_loop(0, 31, body, jnp.zeros((n_e, 1), jnp.int32))
    above = bits > kth
    tied = bits == kth
    need = cap - jnp.sum(above.astype(F32), axis=1, keepdims=True)
    tied_rank = _prefix_count(tied.astype(F32), tri)
    chosen = above | (tied & (tied_rank <= need))
    slot = _prefix_count(chosen.astype(F32), tri) - 1.0
    slot = jnp.where(chosen, slot, -1.0).astype(jnp.int32)
    slot_ref[0] = slot

    j = lax.broadcasted_iota(jnp.int32, (cap, 1), 0)
    for e in range(n_e):
        hit = slot[e:e + 1, :] == j
        gate_ref[0, e] = jnp.sum(jnp.where(hit, aff[e:e + 1, :], 0.0), axis=1, keepdims=True)


def _select(aff, cap):
    b, n_e, n = aff.shape
    return pl.pallas_call(
        functools.partial(_select_kernel, cap),
        grid=(b,),
        in_specs=[pl.BlockSpec((1, n_e, n), lambda bi: (bi, 0, 0))],
        out_specs=[pl.BlockSpec((1, n_e, n), lambda bi: (bi, 0, 0)),
                   pl.BlockSpec((1, n_e, cap, 1), lambda bi: (bi, 0, 0, 0))],
        out_shape=[jax.ShapeDtypeStruct((b, n_e, n), jnp.int32),
                   jax.ShapeDtypeStruct((b, n_e, cap, 1), F32)],
        compiler_params=_cparams(1),
        name="expert_choice_select",
    )(aff)


def _dispatch_kernel(slot_ref, x_ref, o_ref, acc):
    k = pl.program_id(2)
    cap = o_ref.shape[2]
    j = lax.broadcasted_iota(jnp.int32, (cap, 1), 0)
    sel = (slot_ref[0, 0] == j).astype(BF16)
    part = _dot(sel, x_ref[0])

    @pl.when(k == 0)
    def _():
        acc[...] = part

    @pl.when(k > 0)
    def _():
        acc[...] += part

    @pl.when(k == pl.num_programs(2) - 1)
    def _():
        o_ref[0, 0] = acc[...].astype(BF16)


def _dispatch(slot4, b_lat, cap):
    b, n_e, _, n = slot4.shape
    d = b_lat.shape[-1]
    tk = min(n, 512)
    return pl.pallas_call(
        _dispatch_kernel,
        grid=(b, n_e, n // tk),
        in_specs=[pl.BlockSpec((1, 1, 1, tk), lambda bi, e, k: (bi, e, 0, k)),
                  pl.BlockSpec((1, tk, d), lambda bi, e, k: (bi, k, 0))],
        out_specs=pl.BlockSpec((1, 1, cap, d), lambda bi, e, k: (e, bi, 0, 0)),
        out_shape=jax.ShapeDtypeStruct((n_e, b, cap, d), BF16),
        scratch_shapes=[pltpu.VMEM((cap, d), F32)],
        compiler_params=_cparams(3),
        name="moe_dispatch",
    )(slot4, b_lat)


def _expert_up_kernel(x_ref, wg_ref, wu_ref, o_ref):
    x = x_ref[0, 0]
    g = _dot(x, wg_ref[0])
    o_ref[0, 0] = (g * jax.nn.sigmoid(g) * _dot(x, wu_ref[0])).astype(BF16)


def _expert_up(xin, w_gate, w_up):
    n_e, b, cap, d = xin.shape
    f = w_gate.shape[-1]
    tn = min(f, 512)
    return pl.pallas_call(
        _expert_up_kernel,
        grid=(n_e, f // tn, b),
        in_specs=[pl.BlockSpec((1, 1, cap, d), lambda e, j, bi: (e, bi, 0, 0)),
                  pl.BlockSpec((1, d, tn), lambda e, j, bi: (e, 0, j)),
                  pl.BlockSpec((1, d, tn), lambda e, j, bi: (e, 0, j))],
        out_specs=pl.BlockSpec((1, 1, cap, tn), lambda e, j, bi: (e, bi, 0, j)),
        out_shape=jax.ShapeDtypeStruct((n_e, b, cap, f), BF16),
        compiler_params=_cparams(3),
        name="expert_up",
    )(xin, w_gate, w_up)


def _expert_down_kernel(h_ref, w_ref, g_ref, o_ref):
    o_ref[0, 0] = (_dot(h_ref[0, 0], w_ref[0]) * g_ref[0, 0]).astype(BF16)


def _expert_down(hid, w_down, gate):
    n_e, b, cap, f = hid.shape
    d = w_down.shape[-1]
    tn = min(d, 512)
    return pl.pallas_call(
        _expert_down_kernel,
        grid=(n_e, d // tn, b),
        in_specs=[pl.BlockSpec((1, 1, cap, f), lambda e, j, bi: (e, bi, 0, 0)),
                  pl.BlockSpec((1, f, tn), lambda e, j, bi: (e, 0, j)),
                  pl.BlockSpec((1, 1, cap, 1), lambda e, j, bi: (bi, e, 0, 0))],
        out_specs=pl.BlockSpec((1, 1, cap, tn), lambda e, j, bi: (e, bi, 0, j)),
        out_shape=jax.ShapeDtypeStruct((n_e, b, cap, d), BF16),
        compiler_params=_cparams(3),
        name="expert_down",
    )(hid, w_down, gate)


def _combine_kernel(slot_ref, y_ref, h_ref, pfn_ref, g2_ref, o_ref, acc):
    e = pl.program_id(2)
    cap = y_ref.shape[2]
    j = lax.broadcasted_iota(jnp.int32, (cap, 1), 0)
    sel_t = (slot_ref[0, 0] == j).astype(BF16)
    part = _dot_tn(sel_t, y_ref[0, 0])

    @pl.when(e == 0)
    def _():
        acc[...] = part

    @pl.when(e > 0)
    def _():
        acc[...] += part

    @pl.when(e == pl.num_programs(2) - 1)
    def _():
        o_ref[0] = h_ref[0] + g2_ref[0] * _rms(acc[...], pfn_ref[...])


def _combine(slot4, y, h, post_ffn, g2):
    b, n_e, _, n = slot4.shape
    cap, d = y.shape[2], y.shape[3]
    t = min(n, 256)
    return pl.pallas_call(
        _combine_kernel,
        grid=(b, n // t, n_e),
        in_specs=[pl.BlockSpec((1, 1, 1, t), lambda bi, i, e: (bi, e, 0, i)),
                  pl.BlockSpec((1, 1, cap, d), lambda bi, i, e: (e, bi, 0, 0)),
                  pl.BlockSpec((1, t, d), lambda bi, i, e: (bi, i, 0)),
                  pl.BlockSpec((1, d), lambda bi, i, e: (0, 0)),
                  pl.BlockSpec((1, 1, d), lambda bi, i, e: (bi, 0, 0))],
        out_specs=pl.BlockSpec((1, t, d), lambda bi, i, e: (bi, i, 0)),
        out_shape=jax.ShapeDtypeStruct((b, n, d), F32),
        scratch_shapes=[pltpu.VMEM((t, d), F32)],
        compiler_params=_cparams(3),
        name="moe_combine",
    )(slot4, y, h, post_ffn, g2)


def _rope_tables(n):
    t = jnp.arange(n, dtype=jnp.int32)
    pos_row = (t // GRID_W).astype(F32)
    pos_col = (t % GRID_W).astype(F32)
    n_freq = HEAD_DIM // 4
    inv_freq = ROPE_THETA ** (-jnp.arange(n_freq, dtype=F32) / n_freq)
    ang_r = pos_row[:, None] * inv_freq
    ang_c = pos_col[:, None] * inv_freq
    cos = jnp.concatenate([jnp.cos(ang_r)] * 2 + [jnp.cos(ang_c)] * 2, axis=1)
    sin = jnp.concatenate([-jnp.sin(ang_r), jnp.sin(ang_r), -jnp.sin(ang_c), jnp.sin(ang_c)], axis=1)
    return cos, sin


def kernel(x, c, ctx, c_ctx, w_mod, b_mod, pre_mix_norm, post_mix_norm, pre_ffn_norm, post_ffn_norm,
           w_in, q_norm, k_norm, sgu_norm, sgu_w, sgu_b, w_br_attn, w_br_sgu, w_out,
           w_router, w_gate, w_up, w_down):
    b, n, d = x.shape
    n_ctx = ctx.shape[1]
    assert w_mod.shape[0] == 1, "single-layer block"
    n_e = w_router.shape[-1]
    cap = EC_CAPACITY_FACTOR * n // n_e

    rows = -(-(b + 1) // SUBLANES) * SUBLANES
    cc = jnp.zeros((rows, d), F32).at[:b].set(c).at[b].set(c_ctx)
    mod = _mod_vectors(cc, w_mod[0], b_mod)
    sh1, sc1, g1, sh2, sc2, g2 = [mod[:, i * d:(i + 1) * d].reshape(rows, 1, d) for i in range(N_MOD)]

    a_lat = _norm_mod(x, pre_mix_norm, sh1, sc1, lambda bi: bi).reshape(b * n, d)
    a_ctx = _norm_mod(ctx, pre_mix_norm, sh1, sc1, lambda bi: b).reshape(b * n_ctx, d)

    w_in_b = w_in[0].astype(BF16)
    cos, sin = _rope_tables(n)
    q, k, v, u, s, ga, gb = _in_proj(a_lat, w_in_b, cos, sin, q_norm, k_norm, sgu_norm, n, d)
    k_c, v_c = _ctx_kv(a_ctx, w_in_b, k_norm, d)

    attn = _attention(q.reshape(b, n, ATTN_WIDTH), k.reshape(b, n, KV_WIDTH), v.reshape(b, n, KV_WIDTH),
                      k_c.reshape(b, n_ctx, KV_WIDTH), v_c.reshape(b, n_ctx, KV_WIDTH))
    merged = _merge(attn.reshape(b * n, ATTN_WIDTH), u, s, sgu_w[0], sgu_b[0].T,
                    w_br_attn[0].astype(BF16), w_br_sgu[0].astype(BF16), ga, gb, n, d)
    h1, b_lat = _out_proj(merged, w_out[0].astype(BF16), x.reshape(b * n, d), post_mix_norm, g1,
                          pre_ffn_norm, sh2, sc2, n, d)
    h1 = h1.reshape(b, n, d)

    aff = _router(h1, pre_ffn_norm, sh2, sc2, w_router[0].T)
    slot, gate = _select(aff, cap)
    slot4 = slot.reshape(b, n_e, 1, n)
    xin = _dispatch(slot4, b_lat.reshape(b, n, d), cap)
    hid = _expert_up(xin, w_gate[0].astype(BF16), w_up[0].astype(BF16))
    y = _expert_down(hid, w_down[0].astype(BF16), gate)
    return _combine(slot4, y, h1, post_ffn_norm, g2)
```

```python
import functools
import math

import jax
import jax.numpy as jnp
from jax import lax
from jax.experimental import pallas as pl
from jax.experimental.pallas import tpu as pltpu

F32 = jnp.float32
BF16 = jnp.bfloat16

GRID_W = 64
N_HEADS = 16
N_KV_HEADS = 4
HEAD_DIM = 128
GQA_GROUP = N_HEADS // N_KV_HEADS
ATTN_WIDTH = N_HEADS * HEAD_DIM
KV_WIDTH = N_KV_HEADS * HEAD_DIM
ROPE_THETA = 10000.0
SGU_CHUNK = 128
SGU_GROUPS = 8
EC_CAPACITY_FACTOR = 2
NORM_EPS = 1e-6
N_MOD = 6
QK_PRESCALE = HEAD_DIM ** -0.5 * math.log2(math.e)
ATTN_KEY_CHUNK = 512
MOE_PIECE_ROWS = 16
MOE_CHUNK_ROWS = 256
MOE_TOKEN_TILE = 128

LANES = 128
SUBLANES = 8
VMEM_BUDGET_BYTES = 56 * 1024 * 1024


def _cparams(n_axes, vmem=VMEM_BUDGET_BYTES):
    return pltpu.CompilerParams(
        dimension_semantics=("arbitrary",) * n_axes, vmem_limit_bytes=vmem)


def _rms(x, gain):
    return x * lax.rsqrt(jnp.mean(x * x, axis=-1, keepdims=True) + NORM_EPS) * gain


def _dot(a, b):
    return jnp.dot(a, b, preferred_element_type=F32)


def _dot_nt(a, b):
    return lax.dot_general(a, b, (((1,), (1,)), ((), ())), preferred_element_type=F32)


def _dot_tn(a, b):
    return lax.dot_general(a, b, (((0,), (0,)), ((), ())), preferred_element_type=F32)


def _mod_kernel(c_ref, w_ref, b_ref, o_ref):
    c = c_ref[...]
    s = (c * jax.nn.sigmoid(c)).astype(BF16)
    o_ref[...] = _dot(s, w_ref[...].astype(BF16)) + b_ref[...]


def _mod_vectors(cc, w_mod, b_mod):
    rows, d = cc.shape
    n = w_mod.shape[1]
    tn = min(n, 512)
    return pl.pallas_call(
        _mod_kernel,
        grid=(n // tn,),
        in_specs=[pl.BlockSpec((rows, d), lambda j: (0, 0)),
                  pl.BlockSpec((d, tn), lambda j: (0, j)),
                  pl.BlockSpec((1, tn), lambda j: (0, j))],
        out_specs=pl.BlockSpec((rows, tn), lambda j: (0, j)),
        out_shape=jax.ShapeDtypeStruct((rows, n), F32),
        compiler_params=_cparams(1),
        name="mod_vectors",
    )(cc, w_mod, b_mod)


def _norm_mod_kernel(x_ref, g_ref, sh_ref, sc_ref, o_ref):
    x = x_ref[0]
    o_ref[0] = (_rms(x, g_ref[...]) * (1.0 + sc_ref[0]) + sh_ref[0]).astype(BF16)


def _norm_mod(x, gain, shift, scale, row_of_batch):
    b, n, d = x.shape
    tm = min(n, 512)
    return pl.pallas_call(
        _norm_mod_kernel,
        grid=(b, n // tm),
        in_specs=[pl.BlockSpec((1, tm, d), lambda bi, i: (bi, i, 0)),
                  pl.BlockSpec((1, d), lambda bi, i: (0, 0)),
                  pl.BlockSpec((1, 1, d), lambda bi, i: (row_of_batch(bi), 0, 0)),
                  pl.BlockSpec((1, 1, d), lambda bi, i: (row_of_batch(bi), 0, 0))],
        out_specs=pl.BlockSpec((1, tm, d), lambda bi, i: (bi, i, 0)),
        out_shape=jax.ShapeDtypeStruct((b, n, d), BF16),
        compiler_params=_cparams(2),
        name="norm_modulate",
    )(x, gain, shift, scale)


def _head_norm_rope(acc, gain, cos, sin, lane_lo):
    outs = []
    for h in range(acc.shape[1] // HEAD_DIM):
        y = _rms(acc[:, h * HEAD_DIM:(h + 1) * HEAD_DIM], gain)
        if cos is not None:
            partner = jnp.where(lane_lo, pltpu.roll(y, HEAD_DIM - HEAD_DIM // 4, axis=1),
                                pltpu.roll(y, HEAD_DIM // 4, axis=1))
            y = y * cos + partner * sin
        outs.append(y)
    return outs[0] if len(outs) == 1 else jnp.concatenate(outs, axis=1)


def _in_proj_kernel(bounds, group_dim, a_ref, w_ref, cos_ref, sin_ref, qn_ref, kn_ref, sn_ref,
                    q_ref, k_ref, v_ref, u_ref, s_ref, ga_ref, gb_ref):
    j = pl.program_id(1)
    acc = _dot(a_ref[...], w_ref[...])
    bq, bk, bv, bu, bs, bga = bounds
    lane = lax.broadcasted_iota(jnp.int32, (1, HEAD_DIM), 1)
    lane_lo = (lane % (HEAD_DIM // 2)) < (HEAD_DIM // 4)

    @pl.when(j < bq)
    def _():
        q = _head_norm_rope(acc, qn_ref[...], cos_ref[...], sin_ref[...], lane_lo)
        q_ref[...] = (q * QK_PRESCALE).astype(BF16)

    @pl.when((j >= bq) & (j < bk))
    def _():
        k_ref[...] = _head_norm_rope(acc, kn_ref[...], cos_ref[...], sin_ref[...], lane_lo).astype(BF16)

    @pl.when((j >= bk) & (j < bv))
    def _():
        v_ref[...] = acc.astype(BF16)

    @pl.when((j >= bv) & (j < bu))
    def _():
        u_ref[...] = jax.nn.gelu(acc).astype(BF16)

    @pl.when((j >= bu) & (j < bs))
    def _():
        g = jax.nn.gelu(acc)
        gain = sn_ref[...]
        parts = [_rms(g[:, c * group_dim:(c + 1) * group_dim], gain[:, c * group_dim:(c + 1) * group_dim])
                 for c in range(acc.shape[1] // group_dim)]
        s_ref[...] = (parts[0] if len(parts) == 1 else jnp.concatenate(parts, axis=1)).astype(BF16)

    @pl.when((j >= bs) & (j < bga))
    def _():
        ga_ref[...] = jax.nn.sigmoid(acc).astype(BF16)

    @pl.when(j >= bga)
    def _():
        gb_ref[...] = jax.nn.sigmoid(acc).astype(BF16)


def _in_proj(a, w, cos, sin, q_norm, k_norm, sgu_norm, seq, d):
    m = a.shape[0]
    sw = d // 2
    gd = sw // SGU_GROUPS
    tm = min(seq, 512)
    tn = 512
    widths = (ATTN_WIDTH, KV_WIDTH, KV_WIDTH, sw, sw, d, d)
    starts = [0]
    for wd in widths:
        assert wd % tn == 0
        starts.append(starts[-1] + wd // tn)
    bounds = tuple(starts[1:7])
    pos_tiles = seq // tm

    def region(r):
        lo, hi = starts[r], starts[r + 1]
        return lambda i, j: (i, jnp.clip(j - lo, 0, hi - lo - 1))

    s_lo, s_n = starts[4], starts[5] - starts[4]
    out_shapes = [jax.ShapeDtypeStruct((m, wd), BF16) for wd in widths]
    return pl.pallas_call(
        functools.partial(_in_proj_kernel, bounds, gd),
        grid=(m // tm, starts[-1]),
        in_specs=[pl.BlockSpec((tm, d), lambda i, j: (i, 0)),
                  pl.BlockSpec((d, tn), lambda i, j: (0, j)),
                  pl.BlockSpec((tm, HEAD_DIM), lambda i, j: (i % pos_tiles, 0)),
                  pl.BlockSpec((tm, HEAD_DIM), lambda i, j: (i % pos_tiles, 0)),
                  pl.BlockSpec((1, HEAD_DIM), lambda i, j: (0, 0)),
                  pl.BlockSpec((1, HEAD_DIM), lambda i, j: (0, 0)),
                  pl.BlockSpec((1, tn), lambda i, j: (0, jnp.clip(j - s_lo, 0, s_n - 1)))],
        out_specs=[pl.BlockSpec((tm, tn), region(r)) for r in range(7)],
        out_shape=out_shapes,
        compiler_params=_cparams(2),
        name="in_projection",
    )(a, w, cos, sin, q_norm, k_norm, sgu_norm)


def _ctx_kv_kernel(a_ref, wk_ref, wv_ref, kn_ref, k_ref, v_ref):
    a = a_ref[...]
    k_ref[...] = _head_norm_rope(_dot(a, wk_ref[...]), kn_ref[...], None, None, None).astype(BF16)
    v_ref[...] = _dot(a, wv_ref[...]).astype(BF16)


def _ctx_kv(a, w, k_norm, d):
    m = a.shape[0]
    tm = min(m, 512)
    kb = ATTN_WIDTH // KV_WIDTH
    return pl.pallas_call(
        _ctx_kv_kernel,
        grid=(m // tm,),
        in_specs=[pl.BlockSpec((tm, d), lambda i: (i, 0)),
                  pl.BlockSpec((d, KV_WIDTH), lambda i: (0, kb)),
                  pl.BlockSpec((d, KV_WIDTH), lambda i: (0, kb + 1)),
                  pl.BlockSpec((1, HEAD_DIM), lambda i: (0, 0))],
        out_specs=[pl.BlockSpec((tm, KV_WIDTH), lambda i: (i, 0))] * 2,
        out_shape=[jax.ShapeDtypeStruct((m, KV_WIDTH), BF16)] * 2,
        compiler_params=_cparams(1),
        name="context_kv",
    )(a, w, w, k_norm)


def _lane_fold(x, op):
    out = x[:, :LANES]
    for c in range(1, x.shape[1] // LANES):
        out = op(out, x[:, c * LANES:(c + 1) * LANES])
    return out


def _attn_kernel(q_ref, kl_ref, vl_ref, kc_ref, vc_ref, o_ref, s_scr):
    tq = q_ref.shape[1]
    n, n_ctx = kl_ref.shape[1], kc_ref.shape[1]
    ck = min(n, ATTN_KEY_CHUNK)
    chunks = [(kl_ref, vl_ref, c * ck, ck, c * ck) for c in range(n // ck)] + [(kc_ref, vc_ref, 0, n_ctx, n)]

    def scores(g):
        q = q_ref[0, :, g * HEAD_DIM:(g + 1) * HEAD_DIM]
        mvec = None
        for k_ref, _, r0, w, col in chunks:
            s = _dot_nt(q, k_ref[0, r0:r0 + w, :])
            s_scr[g, :, col:col + w] = s
            part = _lane_fold(s, jnp.maximum)
            mvec = part if mvec is None else jnp.maximum(mvec, part)
        return jnp.max(mvec, axis=1, keepdims=True)

    def weighted_values(g, m):
        lvec = jnp.zeros((tq, LANES), F32)
        acc = jnp.zeros((tq, HEAD_DIM), F32)
        for _, v_ref, r0, w, col in chunks:
            p = jnp.exp2(s_scr[g, :, col:col + w] - m)
            lvec = lvec + _lane_fold(p, jnp.add)
            acc = acc + _dot(p.astype(BF16), v_ref[0, r0:r0 + w, :])
        o = acc / jnp.sum(lvec, axis=1, keepdims=True)
        o_ref[0, :, g * HEAD_DIM:(g + 1) * HEAD_DIM] = o.astype(BF16)

    m_prev = None
    for g in range(GQA_GROUP + 1):
        m_cur = scores(g) if g < GQA_GROUP else None
        if g > 0:
            weighted_values(g - 1, m_prev)
        m_prev = m_cur


def _attention(q, k_lat, v_lat, k_ctx, v_ctx):
    b, n, _ = q.shape
    n_ctx = k_ctx.shape[1]
    tq = min(n, 256)
    qw = GQA_GROUP * HEAD_DIM
    assert n % min(n, ATTN_KEY_CHUNK) == 0
    return pl.pallas_call(
        _attn_kernel,
        grid=(b, N_KV_HEADS, n // tq),
        in_specs=[pl.BlockSpec((1, tq, qw), lambda bi, h, i: (bi, i, h)),
                  pl.BlockSpec((1, n, HEAD_DIM), lambda bi, h, i: (bi, 0, h)),
                  pl.BlockSpec((1, n, HEAD_DIM), lambda bi, h, i: (bi, 0, h)),
                  pl.BlockSpec((1, n_ctx, HEAD_DIM), lambda bi, h, i: (bi, 0, h)),
                  pl.BlockSpec((1, n_ctx, HEAD_DIM), lambda bi, h, i: (bi, 0, h))],
        out_specs=pl.BlockSpec((1, tq, qw), lambda bi, h, i: (bi, i, h)),
        out_shape=jax.ShapeDtypeStruct((b, n, ATTN_WIDTH), BF16),
        scratch_shapes=[pltpu.VMEM((GQA_GROUP, tq, n + n_ctx), F32)],
        compiler_params=_cparams(3),
        name="gqa_attention",
    )(q, k_lat, v_lat, k_ctx, v_ctx)


def _merge_kernel(group_dim, attn_ref, u_ref, s_ref, sw_ref, sbt_ref, wa_ref, ws_ref, ga_ref, gb_ref,
                  o_ref, sgu_scr):
    j = pl.program_id(1)
    tm = attn_ref.shape[0]

    @pl.when(j == 0)
    def _():
        for g in range(SGU_GROUPS):
            wg = sw_ref[g].astype(BF16)
            bias = sbt_ref[:, g:g + 1]
            cols = slice(g * group_dim, (g + 1) * group_dim)
            for c in range(tm // SGU_CHUNK):
                rows = slice(c * SGU_CHUNK, (c + 1) * SGU_CHUNK)
                mixed = _dot(wg, s_ref[rows, cols]) + bias
                sgu_scr[rows, cols] = (u_ref[rows, cols].astype(F32) * mixed).astype(BF16)

    ya = _dot(attn_ref[...], wa_ref[...])
    ys = _dot(sgu_scr[...], ws_ref[...])
    o_ref[...] = (ga_ref[...].astype(F32) * ya + gb_ref[...].astype(F32) * ys).astype(BF16)


def _merge(attn, u, s, sgu_w, sgu_bt, w_br_attn, w_br_sgu, ga, gb, seq, d):
    m = attn.shape[0]
    sw = d // 2
    tm = min(seq, 512)
    tn = 512
    return pl.pallas_call(
        functools.partial(_merge_kernel, sw // SGU_GROUPS),
        grid=(m // tm, d // tn),
        in_specs=[pl.BlockSpec((tm, ATTN_WIDTH), lambda i, j: (i, 0)),
                  pl.BlockSpec((tm, sw), lambda i, j: (i, 0)),
                  pl.BlockSpec((tm, sw), lambda i, j: (i, 0)),
                  pl.BlockSpec((SGU_GROUPS, SGU_CHUNK, SGU_CHUNK), lambda i, j: (0, 0, 0)),
                  pl.BlockSpec((SGU_CHUNK, SGU_GROUPS), lambda i, j: (0, 0)),
                  pl.BlockSpec((ATTN_WIDTH, tn), lambda i, j: (0, j)),
                  pl.BlockSpec((sw, tn), lambda i, j: (0, j)),
                  pl.BlockSpec((tm, tn), lambda i, j: (i, j)),
                  pl.BlockSpec((tm, tn), lambda i, j: (i, j))],
        out_specs=pl.BlockSpec((tm, tn), lambda i, j: (i, j)),
        out_shape=jax.ShapeDtypeStruct((m, d), BF16),
        scratch_shapes=[pltpu.VMEM((tm, sw), BF16)],
        compiler_params=_cparams(2),
        name="sgu_branch_merge",
    )(attn, u, s, sgu_w, sgu_bt, w_br_attn, w_br_sgu, ga, gb)


def _out_proj_kernel(m_ref, w_ref, x_ref, pmn_ref, g1_ref, pfn_ref, sh2_ref, sc2_ref,
                     h_ref, b_ref, y_scr):
    j = pl.program_id(1)
    y_scr[j] = _dot(m_ref[...], w_ref[...])

    @pl.when(j == pl.num_programs(1) - 1)
    def _():
        y = jnp.concatenate([y_scr[t] for t in range(y_scr.shape[0])], axis=1)
        h = x_ref[...] + g1_ref[0] * _rms(y, pmn_ref[...])
        h_ref[...] = h
        b_ref[...] = _rms(h, pfn_ref[...]) * (1.0 + sc2_ref[0]) + sh2_ref[0]


def _out_proj(merged, w_out, x, post_mix, g1, pre_ffn, sh2, sc2, seq, d):
    m = merged.shape[0]
    tm = min(seq, 256)
    tn = 512
    tiles_per_batch = seq // tm
    row = lambda i, j: (i // tiles_per_batch, 0, 0)
    return pl.pallas_call(
        _out_proj_kernel,
        grid=(m // tm, d // tn),
        in_specs=[pl.BlockSpec((tm, d), lambda i, j: (i, 0)),
                  pl.BlockSpec((d, tn), lambda i, j: (0, j)),
                  pl.BlockSpec((tm, d), lambda i, j: (i, 0)),
                  pl.BlockSpec((1, d), lambda i, j: (0, 0)),
                  pl.BlockSpec((1, 1, d), row),
                  pl.BlockSpec((1, d), lambda i, j: (0, 0)),
                  pl.BlockSpec((1, 1, d), row),
                  pl.BlockSpec((1, 1, d), row)],
        out_specs=[pl.BlockSpec((tm, d), lambda i, j: (i, 0)),
                   pl.BlockSpec((tm, d), lambda i, j: (i, 0))],
        out_shape=[jax.ShapeDtypeStruct((m, d), F32), jax.ShapeDtypeStruct((m, d), F32)],
        scratch_shapes=[pltpu.VMEM((d // tn, tm, tn), F32)],
        compiler_params=_cparams(2),
        name="out_projection",
    )(merged, w_out, x, post_mix, g1, pre_ffn, sh2, sc2)


def _split_bf16(x):
    hi = x.astype(BF16)
    return hi, (x - hi.astype(F32)).astype(BF16)


def _router_kernel(x_ref, wr_ref, aff_ref):
    n_e = wr_ref.shape[0]
    b_hi, b_lo = _split_bf16(x_ref[0])
    w_hi, w_lo = _split_bf16(wr_ref[...])
    p1 = _dot_nt(jnp.concatenate([w_hi, w_lo], axis=0), b_hi)
    logits = p1[:n_e] + p1[n_e:] + _dot_nt(w_hi, b_lo)
    z = jnp.exp(logits - jnp.max(logits, axis=0, keepdims=True))
    aff_ref[0] = z / jnp.sum(z, axis=0, keepdims=True)


def _router(b_lat, w_router_t):
    b, n, d = b_lat.shape
    n_e = w_router_t.shape[0]
    t = min(n, 512)
    return pl.pallas_call(
        _router_kernel,
        grid=(b, n // t),
        in_specs=[pl.BlockSpec((1, t, d), lambda bi, i: (bi, i, 0)),
                  pl.BlockSpec((n_e, d), lambda bi, i: (0, 0))],
        out_specs=pl.BlockSpec((1, n_e, t), lambda bi, i: (bi, 0, i)),
        out_shape=jax.ShapeDtypeStruct((b, n_e, n), F32),
        compiler_params=_cparams(2),
        name="router_affinity",
    )(b_lat, w_router_t)


def _prefix_count(x01, tri):
    rows, n = x01.shape
    off = jnp.zeros((rows, 1), F32)
    outs = []
    for c in range(n // LANES):
        y = _dot(x01[:, c * LANES:(c + 1) * LANES].astype(BF16), tri) + off
        outs.append(y)
        off = y[:, LANES - 1:LANES]
    return jnp.concatenate(outs, axis=1)


def _select_kernel(cap, tile_shift, aff_ref, slot_ref, gate_ref, idx_ref, lo_ref):
    aff = aff_ref[0]
    n_e, n = aff.shape
    bits = pltpu.bitcast(aff, jnp.int32)
    r = lax.broadcasted_iota(jnp.int32, (LANES, LANES), 0)
    c = lax.broadcasted_iota(jnp.int32, (LANES, LANES), 1)
    tri = (r <= c).astype(BF16)
    strict = (r < c).astype(BF16)

    def body(i, ans):
        cand = ans | lax.shift_left(jnp.int32(1), 30 - i)
        cnt = jnp.sum((bits >= cand).astype(F32), axis=1, keepdims=True)
        return jnp.where(cnt >= cap, cand, ans)

    kth = lax.fori_loop(0, 31, body, jnp.zeros((n_e, 1), jnp.int32))
    above = bits > kth
    tied = bits == kth
    need = cap - jnp.sum(above.astype(F32), axis=1, keepdims=True)
    tied_rank = _prefix_count(tied.astype(F32), tri)
    chosen = above | (tied & (tied_rank <= need))
    slot = _prefix_count(chosen.astype(F32), tri) - 1.0
    slot = jnp.where(chosen, slot, -1.0).astype(jnp.int32)
    slot_ref[0] = slot

    j = lax.broadcasted_iota(jnp.int32, (cap, 1), 0)
    tok = lax.broadcasted_iota(jnp.int32, (1, n), 1).astype(F32)
    for e in range(n_e):
        hit = slot[e:e + 1, :] == j
        gate_ref[0, e] = jnp.sum(jnp.where(hit, aff[e:e + 1, :], 0.0), axis=1, keepdims=True)
        idx_ref[0, e] = jnp.sum(jnp.where(hit, tok, 0.0), axis=1, keepdims=True).astype(jnp.int32)

    t_of = lax.shift_right_logical(lax.broadcasted_iota(jnp.int32, (n, LANES), 0), tile_shift)
    in_tile = (t_of == lax.broadcasted_iota(jnp.int32, (n, LANES), 1)).astype(BF16)
    per_tile = _dot(chosen.astype(BF16), in_tile)
    lo_ref[0] = _dot(per_tile.astype(BF16), strict).astype(jnp.int32)


def _select(aff, cap, tile):
    b, n_e, n = aff.shape
    tile_shift = tile.bit_length() - 1
    assert tile == 1 << tile_shift and tile <= 256 and n // tile < LANES
    return pl.pallas_call(
        functools.partial(_select_kernel, cap, tile_shift),
        grid=(b,),
        in_specs=[pl.BlockSpec((1, n_e, n), lambda bi: (bi, 0, 0))],
        out_specs=[pl.BlockSpec((1, n_e, n), lambda bi: (bi, 0, 0)),
                   pl.BlockSpec((1, n_e, cap, 1), lambda bi: (bi, 0, 0, 0)),
                   pl.BlockSpec((1, n_e, cap, 1), lambda bi: (bi, 0, 0, 0)),
                   pl.BlockSpec((1, n_e, LANES), lambda bi: (bi, 0, 0))],
        out_shape=[jax.ShapeDtypeStruct((b, n_e, n), jnp.int32),
                   jax.ShapeDtypeStruct((b, n_e, cap, 1), F32),
                   jax.ShapeDtypeStruct((b, n_e, cap, 1), jnp.int32),
                   jax.ShapeDtypeStruct((b, n_e, LANES), jnp.int32)],
        compiler_params=_cparams(1),
        name="expert_choice_select",
    )(aff)


def _gather_kernel(n_tok, idx_ref, src_ref, o_ref, buf, sem):
    cap = buf.shape[0]
    base = pl.program_id(1) * n_tok

    def row_copy(j):
        return pltpu.make_async_copy(src_ref.at[pl.ds(base + idx_ref[0, 0, j], 1), :],
                                     buf.at[pl.ds(j, 1), :], sem)

    def issue(j, carry):
        row_copy(j).start()
        return carry

    def drain(j, carry):
        row_copy(j).wait()
        return carry

    lax.fori_loop(0, cap, issue, 0, unroll=8)
    lax.fori_loop(0, cap, drain, 0, unroll=8)
    o_ref[0, 0] = buf[...].astype(BF16)


def _gather(idx, b_lat, n_tok):
    cap = idx.shape[-1]
    d = b_lat.shape[-1]
    b = b_lat.shape[0] // n_tok
    n_e = idx.shape[0] // b
    return pl.pallas_call(
        functools.partial(_gather_kernel, n_tok),
        grid=(n_e, b),
        in_specs=[pl.BlockSpec((1, 1, cap), lambda e, bi: (bi * n_e + e, 0, 0), memory_space=pltpu.SMEM),
                  pl.BlockSpec(memory_space=pl.ANY)],
        out_specs=pl.BlockSpec((1, 1, cap, d), lambda e, bi: (e, bi, 0, 0)),
        out_shape=jax.ShapeDtypeStruct((n_e, b, cap, d), BF16),
        scratch_shapes=[pltpu.VMEM((cap, d), F32), pltpu.SemaphoreType.DMA(())],
        compiler_params=_cparams(2),
        name="moe_gather",
    )(idx, b_lat)


def _expert_up_kernel(x_ref, wg_ref, wu_ref, o_ref):
    x = x_ref[0, 0]
    g = _dot(x, wg_ref[0])
    o_ref[0, 0] = (g * jax.nn.sigmoid(g) * _dot(x, wu_ref[0])).astype(BF16)


def _expert_up(xin, w_gate, w_up):
    n_e, b, cap, d = xin.shape
    f = w_gate.shape[-1]
    tn = min(f, 512)
    return pl.pallas_call(
        _expert_up_kernel,
        grid=(n_e, f // tn, b),
        in_specs=[pl.BlockSpec((1, 1, cap, d), lambda e, j, bi: (e, bi, 0, 0)),
                  pl.BlockSpec((1, d, tn), lambda e, j, bi: (e, 0, j)),
                  pl.BlockSpec((1, d, tn), lambda e, j, bi: (e, 0, j))],
        out_specs=pl.BlockSpec((1, 1, cap, tn), lambda e, j, bi: (e, bi, 0, j)),
        out_shape=jax.ShapeDtypeStruct((n_e, b, cap, f), BF16),
        compiler_params=_cparams(3),
        name="expert_up",
    )(xin, w_gate, w_up)


def _expert_down_kernel(h_ref, w_ref, g_ref, o_ref):
    o_ref[0, 0] = (_dot(h_ref[0, 0], w_ref[0]) * g_ref[0, 0]).astype(BF16)


def _expert_down(hid, w_down, gate):
    n_e, b, cap, f = hid.shape
    d = w_down.shape[-1]
    tn = min(d, 512)
    return pl.pallas_call(
        _expert_down_kernel,
        grid=(n_e, d // tn, b),
        in_specs=[pl.BlockSpec((1, 1, cap, f), lambda e, j, bi: (e, bi, 0, 0)),
                  pl.BlockSpec((1, f, tn), lambda e, j, bi: (e, 0, j)),
                  pl.BlockSpec((1, 1, cap, 1), lambda e, j, bi: (bi, e, 0, 0))],
        out_specs=pl.BlockSpec((1, 1, cap, tn), lambda e, j, bi: (e, bi, 0, j)),
        out_shape=jax.ShapeDtypeStruct((n_e, b, cap, d), BF16),
        compiler_params=_cparams(3),
        name="expert_down",
    )(hid, w_down, gate)


def _combine_kernel(n_tiles, lo_ref, slot_ref, y_ref, h_ref, pfn_ref, g2_ref, o_ref, buf, acc, sem):
    bi, i = pl.program_id(0), pl.program_id(1)
    n_e, t = slot_ref.shape[1], slot_ref.shape[2]
    piece, chunk = MOE_PIECE_ROWS, MOE_CHUNK_ROWS
    shift = piece.bit_length() - 1

    @pl.when((bi == 0) & (i == 0))
    def _():
        buf[...] = jnp.zeros_like(buf)

    def piece_copy(e, src_piece, dst_row):
        return pltpu.make_async_copy(
            y_ref.at[e, bi, pl.ds(pl.multiple_of(src_piece * piece, piece), piece), :],
            buf.at[pl.ds(pl.multiple_of(dst_row, piece), piece), :], sem)

    filled = jnp.int32(0)
    shifts = []
    for e in range(n_e):
        row = (bi * n_e + e) * (n_tiles + 1) + i
        lo, hi = lo_ref[row], lo_ref[row + 1]
        first = lax.shift_right_logical(lo, shift)
        count = jnp.where(hi > lo, lax.shift_right_logical(hi + (piece - 1), shift) - first, 0)

        def issue(p, carry, e=e, first=first, filled=filled):
            piece_copy(e, first + p, filled + p * piece).start()
            return carry

        lax.fori_loop(0, count, issue, 0)
        shifts.append(filled - first * piece)
        filled = filled + count * piece

    def drain(p, carry):
        piece_copy(0, 0, 0).wait()
        return carry

    lax.fori_loop(0, lax.shift_right_logical(filled, shift), drain, 0)

    slot = slot_ref[0]
    target = jnp.concatenate(
        [jnp.where(slot[e:e + 1, :] >= 0, slot[e:e + 1, :] + shifts[e], -1) for e in range(n_e)], axis=0)

    acc[...] = jnp.zeros_like(acc)

    def add_chunk(k, carry):
        r0 = pl.multiple_of(k * chunk, chunk)
        rows = lax.broadcasted_iota(jnp.int32, (chunk, 1), 0) + r0
        hit = target[0:1, :] == rows
        for e in range(1, n_e):
            hit = hit | (target[e:e + 1, :] == rows)
        acc[...] += _dot_tn(hit.astype(BF16), buf[pl.ds(r0, chunk), :])
        return carry

    lax.fori_loop(0, lax.shift_right_logical(filled + (chunk - 1), chunk.bit_length() - 1), add_chunk, 0)
    o_ref[0] = h_ref[0] + g2_ref[0] * _rms(acc[...], pfn_ref[...])


def _combine(lo, slot, y, h, post_ffn, g2, tile):
    b, n_e, n = slot.shape
    cap, d = y.shape[2], y.shape[3]
    n_tiles = n // tile
    assert cap % MOE_PIECE_ROWS == 0 and MOE_CHUNK_ROWS % MOE_PIECE_ROWS == 0
    buf_rows = n_e * (tile + 2 * MOE_PIECE_ROWS)
    buf_rows = -(-buf_rows // MOE_CHUNK_ROWS) * MOE_CHUNK_ROWS
    return pl.pallas_call(
        functools.partial(_combine_kernel, n_tiles),
        grid_spec=pltpu.PrefetchScalarGridSpec(
            num_scalar_prefetch=1,
            grid=(b, n_tiles),
            in_specs=[pl.BlockSpec((1, n_e, tile), lambda bi, i, lo_ref: (bi, 0, i)),
                      pl.BlockSpec(memory_space=pl.ANY),
                      pl.BlockSpec((1, tile, d), lambda bi, i, lo_ref: (bi, i, 0)),
                      pl.BlockSpec((1, d), lambda bi, i, lo_ref: (0, 0)),
                      pl.BlockSpec((1, 1, d), lambda bi, i, lo_ref: (bi, 0, 0))],
            out_specs=pl.BlockSpec((1, tile, d), lambda bi, i, lo_ref: (bi, i, 0)),
            scratch_shapes=[pltpu.VMEM((buf_rows, d), BF16), pltpu.VMEM((tile, d), F32),
                            pltpu.SemaphoreType.DMA(())]),
        out_shape=jax.ShapeDtypeStruct((b, n, d), F32),
        compiler_params=_cparams(2),
        name="moe_combine",
    )(lo, slot, y, h, post_ffn, g2)


def _rope_tables(n):
    t = jnp.arange(n, dtype=jnp.int32)
    pos_row = (t // GRID_W).astype(F32)
    pos_col = (t % GRID_W).astype(F32)
    n_freq = HEAD_DIM // 4
    inv_freq = ROPE_THETA ** (-jnp.arange(n_freq, dtype=F32) / n_freq)
    ang_r = pos_row[:, None] * inv_freq
    ang_c = pos_col[:, None] * inv_freq
    cos = jnp.concatenate([jnp.cos(ang_r)] * 2 + [jnp.cos(ang_c)] * 2, axis=1)
    sin = jnp.concatenate([-jnp.sin(ang_r), jnp.sin(ang_r), -jnp.sin(ang_c), jnp.sin(ang_c)], axis=1)
    return cos, sin


def kernel(x, c, ctx, c_ctx, w_mod, b_mod, pre_mix_norm, post_mix_norm, pre_ffn_norm, post_ffn_norm,
           w_in, q_norm, k_norm, sgu_norm, sgu_w, sgu_b, w_br_attn, w_br_sgu, w_out,
           w_router, w_gate, w_up, w_down):
    b, n, d = x.shape
    n_ctx = ctx.shape[1]
    assert w_mod.shape[0] == 1, "single-layer block"
    n_e = w_router.shape[-1]
    cap = EC_CAPACITY_FACTOR * n // n_e

    rows = -(-(b + 1) // SUBLANES) * SUBLANES
    cc = jnp.zeros((rows, d), F32).at[:b].set(c).at[b].set(c_ctx)
    mod = _mod_vectors(cc, w_mod[0], b_mod)
    sh1, sc1, g1, sh2, sc2, g2 = [mod[:, i * d:(i + 1) * d].reshape(rows, 1, d) for i in range(N_MOD)]

    a_lat = _norm_mod(x, pre_mix_norm, sh1, sc1, lambda bi: bi).reshape(b * n, d)
    a_ctx = _norm_mod(ctx, pre_mix_norm, sh1, sc1, lambda bi: b).reshape(b * n_ctx, d)

    w_in_b = w_in[0].astype(BF16)
    cos, sin = _rope_tables(n)
    q, k, v, u, s, ga, gb = _in_proj(a_lat, w_in_b, cos, sin, q_norm, k_norm, sgu_norm, n, d)
    k_c, v_c = _ctx_kv(a_ctx, w_in_b, k_norm, d)

    attn = _attention(q.reshape(b, n, ATTN_WIDTH), k.reshape(b, n, KV_WIDTH), v.reshape(b, n, KV_WIDTH),
                      k_c.reshape(b, n_ctx, KV_WIDTH), v_c.reshape(b, n_ctx, KV_WIDTH))
    merged = _merge(attn.reshape(b * n, ATTN_WIDTH), u, s, sgu_w[0], sgu_b[0].T,
                    w_br_attn[0].astype(BF16), w_br_sgu[0].astype(BF16), ga, gb, n, d)
    h1, b_lat = _out_proj(merged, w_out[0].astype(BF16), x.reshape(b * n, d), post_mix_norm, g1,
                          pre_ffn_norm, sh2, sc2, n, d)
    h1 = h1.reshape(b, n, d)

    tile = min(n, MOE_TOKEN_TILE)
    aff = _router(b_lat.reshape(b, n, d), w_router[0].T)
    slot, gate, idx, lo = _select(aff, cap, tile)
    xin = _gather(idx.reshape(b * n_e, 1, cap), b_lat, n)
    hid = _expert_up(xin, w_gate[0].astype(BF16), w_up[0].astype(BF16))
    y = _expert_down(hid, w_down[0].astype(BF16), gate)
    lo_flat = lo[:, :, :n // tile + 1].reshape(-1)
    return _combine(lo_flat, slot, y, h1, post_ffn_norm, g2, tile)
```

```python
import functools
import math

import jax
import jax.numpy as jnp
from jax import lax
from jax.experimental import pallas as pl
from jax.experimental.pallas import tpu as pltpu

F32 = jnp.float32
BF16 = jnp.bfloat16

GRID_W = 64
N_HEADS = 16
N_KV_HEADS = 4
HEAD_DIM = 128
GQA_GROUP = N_HEADS // N_KV_HEADS
ATTN_WIDTH = N_HEADS * HEAD_DIM
KV_WIDTH = N_KV_HEADS * HEAD_DIM
ROPE_THETA = 10000.0
SGU_CHUNK = 128
SGU_GROUPS = 8
EC_CAPACITY_FACTOR = 2
NORM_EPS = 1e-6
N_MOD = 6
QK_PRESCALE = HEAD_DIM ** -0.5 * math.log2(math.e)
ATTN_KEY_CHUNK = 512
MOE_PIECE_ROWS = 16
MOE_CHUNK_ROWS = 256
MOE_TOKEN_TILE = 128

LANES = 128
SUBLANES = 8
VMEM_BUDGET_BYTES = 56 * 1024 * 1024


def _cparams(n_axes, vmem=VMEM_BUDGET_BYTES):
    return pltpu.CompilerParams(
        dimension_semantics=("arbitrary",) * n_axes, vmem_limit_bytes=vmem)


def _rms(x, gain):
    return x * lax.rsqrt(jnp.mean(x * x, axis=-1, keepdims=True) + NORM_EPS) * gain


def _dot(a, b):
    return jnp.dot(a, b, preferred_element_type=F32)


def _dot_nt(a, b):
    return lax.dot_general(a, b, (((1,), (1,)), ((), ())), preferred_element_type=F32)


def _dot_tn(a, b):
    return lax.dot_general(a, b, (((0,), (0,)), ((), ())), preferred_element_type=F32)


def _mod_kernel(c_ref, w_ref, b_ref, o_ref):
    c = c_ref[...]
    s = (c * jax.nn.sigmoid(c)).astype(BF16)
    o_ref[...] = _dot(s, w_ref[...].astype(BF16)) + b_ref[...]


def _mod_vectors(cc, w_mod, b_mod):
    rows, d = cc.shape
    n = w_mod.shape[1]
    tn = min(n, 512)
    return pl.pallas_call(
        _mod_kernel,
        grid=(n // tn,),
        in_specs=[pl.BlockSpec((rows, d), lambda j: (0, 0)),
                  pl.BlockSpec((d, tn), lambda j: (0, j)),
                  pl.BlockSpec((1, tn), lambda j: (0, j))],
        out_specs=pl.BlockSpec((rows, tn), lambda j: (0, j)),
        out_shape=jax.ShapeDtypeStruct((rows, n), F32),
        compiler_params=_cparams(1),
        name="mod_vectors",
    )(cc, w_mod, b_mod)


def _norm_mod_kernel(x_ref, g_ref, sh_ref, sc_ref, o_ref):
    x = x_ref[0]
    o_ref[0] = (_rms(x, g_ref[...]) * (1.0 + sc_ref[0]) + sh_ref[0]).astype(BF16)


def _norm_mod(x, gain, shift, scale, row_of_batch):
    b, n, d = x.shape
    tm = min(n, 512)
    return pl.pallas_call(
        _norm_mod_kernel,
        grid=(b, n // tm),
        in_specs=[pl.BlockSpec((1, tm, d), lambda bi, i: (bi, i, 0)),
                  pl.BlockSpec((1, d), lambda bi, i: (0, 0)),
                  pl.BlockSpec((1, 1, d), lambda bi, i: (row_of_batch(bi), 0, 0)),
                  pl.BlockSpec((1, 1, d), lambda bi, i: (row_of_batch(bi), 0, 0))],
        out_specs=pl.BlockSpec((1, tm, d), lambda bi, i: (bi, i, 0)),
        out_shape=jax.ShapeDtypeStruct((b, n, d), BF16),
        compiler_params=_cparams(2),
        name="norm_modulate",
    )(x, gain, shift, scale)


def _head_norm_rope(acc, gain, cos, sin, lane_lo):
    outs = []
    for h in range(acc.shape[1] // HEAD_DIM):
        y = _rms(acc[:, h * HEAD_DIM:(h + 1) * HEAD_DIM], gain)
        if cos is not None:
            partner = jnp.where(lane_lo, pltpu.roll(y, HEAD_DIM - HEAD_DIM // 4, axis=1),
                                pltpu.roll(y, HEAD_DIM // 4, axis=1))
            y = y * cos + partner * sin
        outs.append(y)
    return outs[0] if len(outs) == 1 else jnp.concatenate(outs, axis=1)


def _in_proj_kernel(bounds, group_dim, a_ref, w_ref, cos_ref, sin_ref, qn_ref, kn_ref, sn_ref,
                    q_ref, k_ref, v_ref, u_ref, s_ref, ga_ref, gb_ref):
    j = pl.program_id(1)
    acc = _dot(a_ref[...], w_ref[...])
    bq, bk, bv, bu, bs, bga = bounds
    lane = lax.broadcasted_iota(jnp.int32, (1, HEAD_DIM), 1)
    lane_lo = (lane % (HEAD_DIM // 2)) < (HEAD_DIM // 4)

    @pl.when(j < bq)
    def _():
        q = _head_norm_rope(acc, qn_ref[...], cos_ref[...], sin_ref[...], lane_lo)
        q_ref[...] = (q * QK_PRESCALE).astype(BF16)

    @pl.when((j >= bq) & (j < bk))
    def _():
        k_ref[...] = _head_norm_rope(acc, kn_ref[...], cos_ref[...], sin_ref[...], lane_lo).astype(BF16)

    @pl.when((j >= bk) & (j < bv))
    def _():
        v_ref[...] = acc.astype(BF16)

    @pl.when((j >= bv) & (j < bu))
    def _():
        u_ref[...] = jax.nn.gelu(acc).astype(BF16)

    @pl.when((j >= bu) & (j < bs))
    def _():
        g = jax.nn.gelu(acc)
        gain = sn_ref[...]
        parts = [_rms(g[:, c * group_dim:(c + 1) * group_dim], gain[:, c * group_dim:(c + 1) * group_dim])
                 for c in range(acc.shape[1] // group_dim)]
        s_ref[...] = (parts[0] if len(parts) == 1 else jnp.concatenate(parts, axis=1)).astype(BF16)

    @pl.when((j >= bs) & (j < bga))
    def _():
        ga_ref[...] = jax.nn.sigmoid(acc).astype(BF16)

    @pl.when(j >= bga)
    def _():
        gb_ref[...] = jax.nn.sigmoid(acc).astype(BF16)


def _in_proj(a, w, cos, sin, q_norm, k_norm, sgu_norm, seq, d):
    m = a.shape[0]
    sw = d // 2
    gd = sw // SGU_GROUPS
    tm = min(seq, 1024)
    tn = 512
    widths = (ATTN_WIDTH, KV_WIDTH, KV_WIDTH, sw, sw, d, d)
    starts = [0]
    for wd in widths:
        assert wd % tn == 0
        starts.append(starts[-1] + wd // tn)
    bounds = tuple(starts[1:7])
    pos_tiles = seq // tm

    def region(r):
        lo, hi = starts[r], starts[r + 1]
        return lambda i, j: (i, jnp.clip(j - lo, 0, hi - lo - 1))

    s_lo, s_n = starts[4], starts[5] - starts[4]
    out_shapes = [jax.ShapeDtypeStruct((m, wd), BF16) for wd in widths]
    return pl.pallas_call(
        functools.partial(_in_proj_kernel, bounds, gd),
        grid=(m // tm, starts[-1]),
        in_specs=[pl.BlockSpec((tm, d), lambda i, j: (i, 0)),
                  pl.BlockSpec((d, tn), lambda i, j: (0, j)),
                  pl.BlockSpec((tm, HEAD_DIM), lambda i, j: (i % pos_tiles, 0)),
                  pl.BlockSpec((tm, HEAD_DIM), lambda i, j: (i % pos_tiles, 0)),
                  pl.BlockSpec((1, HEAD_DIM), lambda i, j: (0, 0)),
                  pl.BlockSpec((1, HEAD_DIM), lambda i, j: (0, 0)),
                  pl.BlockSpec((1, tn), lambda i, j: (0, jnp.clip(j - s_lo, 0, s_n - 1)))],
        out_specs=[pl.BlockSpec((tm, tn), region(r)) for r in range(7)],
        out_shape=out_shapes,
        compiler_params=_cparams(2),
        name="in_projection",
    )(a, w, cos, sin, q_norm, k_norm, sgu_norm)


def _ctx_kv_kernel(a_ref, wk_ref, wv_ref, kn_ref, k_ref, v_ref):
    a = a_ref[...]
    k_ref[...] = _head_norm_rope(_dot(a, wk_ref[...]), kn_ref[...], None, None, None).astype(BF16)
    v_ref[...] = _dot(a, wv_ref[...]).astype(BF16)


def _ctx_kv(a, w, k_norm, d):
    m = a.shape[0]
    tm = min(m, 512)
    kb = ATTN_WIDTH // KV_WIDTH
    return pl.pallas_call(
        _ctx_kv_kernel,
        grid=(m // tm,),
        in_specs=[pl.BlockSpec((tm, d), lambda i: (i, 0)),
                  pl.BlockSpec((d, KV_WIDTH), lambda i: (0, kb)),
                  pl.BlockSpec((d, KV_WIDTH), lambda i: (0, kb + 1)),
                  pl.BlockSpec((1, HEAD_DIM), lambda i: (0, 0))],
        out_specs=[pl.BlockSpec((tm, KV_WIDTH), lambda i: (i, 0))] * 2,
        out_shape=[jax.ShapeDtypeStruct((m, KV_WIDTH), BF16)] * 2,
        compiler_params=_cparams(1),
        name="context_kv",
    )(a, w, w, k_norm)


def _lane_fold(x, op):
    out = x[:, :LANES]
    for c in range(1, x.shape[1] // LANES):
        out = op(out, x[:, c * LANES:(c + 1) * LANES])
    return out


def _attn_kernel(q_ref, kl_ref, vl_ref, kc_ref, vc_ref, o_ref, s_scr):
    tq = q_ref.shape[1]
    n, n_ctx = kl_ref.shape[1], kc_ref.shape[1]
    ck = min(n, ATTN_KEY_CHUNK)
    chunks = [(kl_ref, vl_ref, c * ck, ck, c * ck) for c in range(n // ck)] + [(kc_ref, vc_ref, 0, n_ctx, n)]

    def scores(g):
        q = q_ref[0, :, g * HEAD_DIM:(g + 1) * HEAD_DIM]
        mvec = None
        for k_ref, _, r0, w, col in chunks:
            s = _dot_nt(q, k_ref[0, r0:r0 + w, :])
            s_scr[g, :, col:col + w] = s
            part = _lane_fold(s, jnp.maximum)
            mvec = part if mvec is None else jnp.maximum(mvec, part)
        return jnp.max(mvec, axis=1, keepdims=True)

    def weighted_values(g, m):
        lvec = jnp.zeros((tq, LANES), F32)
        acc = jnp.zeros((tq, HEAD_DIM), F32)
        for _, v_ref, r0, w, col in chunks:
            p = jnp.exp2(s_scr[g, :, col:col + w] - m)
            lvec = lvec + _lane_fold(p, jnp.add)
            acc = acc + _dot(p.astype(BF16), v_ref[0, r0:r0 + w, :])
        o = acc / jnp.sum(lvec, axis=1, keepdims=True)
        o_ref[0, :, g * HEAD_DIM:(g + 1) * HEAD_DIM] = o.astype(BF16)

    m_prev = None
    for g in range(GQA_GROUP + 1):
        m_cur = scores(g) if g < GQA_GROUP else None
        if g > 0:
            weighted_values(g - 1, m_prev)
        m_prev = m_cur


def _attention(q, k_lat, v_lat, k_ctx, v_ctx):
    b, n, _ = q.shape
    n_ctx = k_ctx.shape[1]
    tq = min(n, 256)
    qw = GQA_GROUP * HEAD_DIM
    assert n % min(n, ATTN_KEY_CHUNK) == 0
    return pl.pallas_call(
        _attn_kernel,
        grid=(b, N_KV_HEADS, n // tq),
        in_specs=[pl.BlockSpec((1, tq, qw), lambda bi, h, i: (bi, i, h)),
                  pl.BlockSpec((1, n, HEAD_DIM), lambda bi, h, i: (bi, 0, h)),
                  pl.BlockSpec((1, n, HEAD_DIM), lambda bi, h, i: (bi, 0, h)),
                  pl.BlockSpec((1, n_ctx, HEAD_DIM), lambda bi, h, i: (bi, 0, h)),
                  pl.BlockSpec((1, n_ctx, HEAD_DIM), lambda bi, h, i: (bi, 0, h))],
        out_specs=pl.BlockSpec((1, tq, qw), lambda bi, h, i: (bi, i, h)),
        out_shape=jax.ShapeDtypeStruct((b, n, ATTN_WIDTH), BF16),
        scratch_shapes=[pltpu.VMEM((GQA_GROUP, tq, n + n_ctx), F32)],
        compiler_params=_cparams(3),
        name="gqa_attention",
    )(q, k_lat, v_lat, k_ctx, v_ctx)


def _merge_kernel(group_dim, attn_ref, u_ref, s_ref, sw_ref, sbt_ref, wa_ref, ws_ref, ga_ref, gb_ref,
                  o_ref, sgu_scr):
    j = pl.program_id(1)
    tm = attn_ref.shape[0]

    @pl.when(j == 0)
    def _():
        for g in range(SGU_GROUPS):
            wg = sw_ref[g].astype(BF16)
            bias = sbt_ref[:, g:g + 1]
            cols = slice(g * group_dim, (g + 1) * group_dim)
            for c in range(tm // SGU_CHUNK):
                rows = slice(c * SGU_CHUNK, (c + 1) * SGU_CHUNK)
                mixed = _dot(wg, s_ref[rows, cols]) + bias
                sgu_scr[rows, cols] = (u_ref[rows, cols].astype(F32) * mixed).astype(BF16)

    ya = _dot(attn_ref[...], wa_ref[...])
    ys = _dot(sgu_scr[...], ws_ref[...])
    o_ref[...] = (ga_ref[...].astype(F32) * ya + gb_ref[...].astype(F32) * ys).astype(BF16)


def _merge(attn, u, s, sgu_w, sgu_bt, w_br_attn, w_br_sgu, ga, gb, seq, d):
    m = attn.shape[0]
    sw = d // 2
    tm = min(seq, 512)
    tn = 512
    return pl.pallas_call(
        functools.partial(_merge_kernel, sw // SGU_GROUPS),
        grid=(m // tm, d // tn),
        in_specs=[pl.BlockSpec((tm, ATTN_WIDTH), lambda i, j: (i, 0)),
                  pl.BlockSpec((tm, sw), lambda i, j: (i, 0)),
                  pl.BlockSpec((tm, sw), lambda i, j: (i, 0)),
                  pl.BlockSpec((SGU_GROUPS, SGU_CHUNK, SGU_CHUNK), lambda i, j: (0, 0, 0)),
                  pl.BlockSpec((SGU_CHUNK, SGU_GROUPS), lambda i, j: (0, 0)),
                  pl.BlockSpec((ATTN_WIDTH, tn), lambda i, j: (0, j)),
                  pl.BlockSpec((sw, tn), lambda i, j: (0, j)),
                  pl.BlockSpec((tm, tn), lambda i, j: (i, j)),
                  pl.BlockSpec((tm, tn), lambda i, j: (i, j))],
        out_specs=pl.BlockSpec((tm, tn), lambda i, j: (i, j)),
        out_shape=jax.ShapeDtypeStruct((m, d), BF16),
        scratch_shapes=[pltpu.VMEM((tm, sw), BF16)],
        compiler_params=_cparams(2),
        name="sgu_branch_merge",
    )(attn, u, s, sgu_w, sgu_bt, w_br_attn, w_br_sgu, ga, gb)


def _out_proj_kernel(m_ref, w_ref, x_hbm, pmn_ref, g1_ref, pfn_ref, sh2_ref, sc2_ref,
                     h_hbm, b_hbm, y_scr, x_buf, h_buf, b_buf, sems):
    i, j = pl.program_id(0), pl.program_id(1)
    last_i, last_j = pl.num_programs(0) - 1, pl.num_programs(1) - 1
    tm = m_ref.shape[0]

    def rows(t):
        return pl.ds(pl.multiple_of(t * tm, tm), tm)

    def x_read():
        return pltpu.make_async_copy(x_hbm.at[rows(i), :], x_buf, sems.at[0])

    def h_write(t):
        return pltpu.make_async_copy(h_buf, h_hbm.at[rows(t), :], sems.at[1])

    def b_write(t):
        return pltpu.make_async_copy(b_buf, b_hbm.at[rows(t), :], sems.at[2])

    @pl.when(j == 0)
    def _():
        x_read().start()

    y_scr[j] = _dot(m_ref[...], w_ref[...])

    @pl.when(j == last_j)
    def _():
        x_read().wait()

        @pl.when(i > 0)
        def _():
            h_write(i - 1).wait()
            b_write(i - 1).wait()

        n_t, _, tn = y_scr.shape
        d = n_t * tn
        col = lambda t: slice(t * tn, (t + 1) * tn)
        ssq = sum(jnp.sum(y_scr[t] * y_scr[t], axis=1, keepdims=True) for t in range(n_t))
        y_inv = lax.rsqrt(ssq / d + NORM_EPS)
        ssq = jnp.zeros_like(ssq)
        for t in range(n_t):
            h = x_buf[:, col(t)] + g1_ref[0, :, col(t)] * (y_scr[t] * y_inv * pmn_ref[:, col(t)])
            h_buf[:, col(t)] = h
            ssq = ssq + jnp.sum(h * h, axis=1, keepdims=True)
        h_inv = lax.rsqrt(ssq / d + NORM_EPS)
        for t in range(n_t):
            b_buf[:, col(t)] = (h_buf[:, col(t)] * h_inv * pfn_ref[:, col(t)] * (1.0 + sc2_ref[0, :, col(t)])
                                + sh2_ref[0, :, col(t)])
        h_write(i).start()
        b_write(i).start()

        @pl.when(i == last_i)
        def _():
            h_write(i).wait()
            b_write(i).wait()


def _out_proj(merged, w_out, x, post_mix, g1, pre_ffn, sh2, sc2, seq, d):
    m = merged.shape[0]
    tm = min(seq, 512)
    tn = 512
    tiles_per_batch = seq // tm
    row = lambda i, j: (i // tiles_per_batch, 0, 0)
    any_spec = pl.BlockSpec(memory_space=pl.ANY)
    return pl.pallas_call(
        _out_proj_kernel,
        grid=(m // tm, d // tn),
        in_specs=[pl.BlockSpec((tm, d), lambda i, j: (i, 0)),
                  pl.BlockSpec((d, tn), lambda i, j: (0, j)),
                  any_spec,
                  pl.BlockSpec((1, d), lambda i, j: (0, 0)),
                  pl.BlockSpec((1, 1, d), row),
                  pl.BlockSpec((1, d), lambda i, j: (0, 0)),
                  pl.BlockSpec((1, 1, d), row),
                  pl.BlockSpec((1, 1, d), row)],
        out_specs=[any_spec, any_spec],
        out_shape=[jax.ShapeDtypeStruct((m, d), F32), jax.ShapeDtypeStruct((m, d), F32)],
        scratch_shapes=[pltpu.VMEM((d // tn, tm, tn), F32), pltpu.VMEM((tm, d), F32),
                        pltpu.VMEM((tm, d), F32), pltpu.VMEM((tm, d), F32),
                        pltpu.SemaphoreType.DMA((3,))],
        compiler_params=_cparams(2),
        name="out_projection",
    )(merged, w_out, x, post_mix, g1, pre_ffn, sh2, sc2)


def _split_bf16(x):
    hi = x.astype(BF16)
    return hi, (x - hi.astype(F32)).astype(BF16)


def _router_kernel(x_ref, wr_ref, aff_ref):
    n_e = wr_ref.shape[0]
    b_hi, b_lo = _split_bf16(x_ref[0])
    w_hi, w_lo = _split_bf16(wr_ref[...])
    p1 = _dot_nt(jnp.concatenate([w_hi, w_lo], axis=0), b_hi)
    logits = p1[:n_e] + p1[n_e:] + _dot_nt(w_hi, b_lo)
    z = jnp.exp(logits - jnp.max(logits, axis=0, keepdims=True))
    aff_ref[0] = z / jnp.sum(z, axis=0, keepdims=True)


def _router(b_lat, w_router_t):
    b, n, d = b_lat.shape
    n_e = w_router_t.shape[0]
    t = min(n, 512)
    return pl.pallas_call(
        _router_kernel,
        grid=(b, n // t),
        in_specs=[pl.BlockSpec((1, t, d), lambda bi, i: (bi, i, 0)),
                  pl.BlockSpec((n_e, d), lambda bi, i: (0, 0))],
        out_specs=pl.BlockSpec((1, n_e, t), lambda bi, i: (bi, 0, i)),
        out_shape=jax.ShapeDtypeStruct((b, n_e, n), F32),
        compiler_params=_cparams(2),
        name="router_affinity",
    )(b_lat, w_router_t)


def _prefix_count(x01, tri):
    rows, n = x01.shape
    off = jnp.zeros((rows, 1), F32)
    outs = []
    for c in range(n // LANES):
        y = _dot(x01[:, c * LANES:(c + 1) * LANES].astype(BF16), tri) + off
        outs.append(y)
        off = y[:, LANES - 1:LANES]
    return jnp.concatenate(outs, axis=1)


def _select_kernel(cap, tile_shift, aff_ref, slot_ref, gate_ref, idx_ref, lo_ref):
    aff = aff_ref[0]
    n_e, n = aff.shape
    bits = pltpu.bitcast(aff, jnp.int32)
    r = lax.broadcasted_iota(jnp.int32, (LANES, LANES), 0)
    c = lax.broadcasted_iota(jnp.int32, (LANES, LANES), 1)
    tri = (r <= c).astype(BF16)
    strict = (r < c).astype(BF16)

    def body(i, ans):
        cand = ans | lax.shift_left(jnp.int32(1), 30 - i)
        cnt = jnp.sum((bits >= cand).astype(F32), axis=1, keepdims=True)
        return jnp.where(cnt >= cap, cand, ans)

    kth = lax.fori_loop(0, 31, body, jnp.zeros((n_e, 1), jnp.int32))
    above = bits > kth
    tied = bits == kth
    need = cap - jnp.sum(above.astype(F32), axis=1, keepdims=True)
    tied_rank = _prefix_count(tied.astype(F32), tri)
    chosen = above | (tied & (tied_rank <= need))
    slot = _prefix_count(chosen.astype(F32), tri) - 1.0
    slot = jnp.where(chosen, slot, -1.0).astype(jnp.int32)
    slot_ref[0] = slot

    j = lax.broadcasted_iota(jnp.int32, (cap, 1), 0)
    tok = lax.broadcasted_iota(jnp.int32, (1, n), 1).astype(F32)
    for e in range(n_e):
        hit = slot[e:e + 1, :] == j
        gate_ref[0, e] = jnp.sum(jnp.where(hit, aff[e:e + 1, :], 0.0), axis=1, keepdims=True)
        idx_ref[0, e] = jnp.sum(jnp.where(hit, tok, 0.0), axis=1, keepdims=True).astype(jnp.int32)

    t_of = lax.shift_right_logical(lax.broadcasted_iota(jnp.int32, (n, LANES), 0), tile_shift)
    in_tile = (t_of == lax.broadcasted_iota(jnp.int32, (n, LANES), 1)).astype(BF16)
    per_tile = _dot(chosen.astype(BF16), in_tile)
    lo_ref[0] = _dot(per_tile.astype(BF16), strict).astype(jnp.int32)


def _select(aff, cap, tile):
    b, n_e, n = aff.shape
    tile_shift = tile.bit_length() - 1
    assert tile == 1 << tile_shift and tile <= 256 and n // tile < LANES
    return pl.pallas_call(
        functools.partial(_select_kernel, cap, tile_shift),
        grid=(b,),
        in_specs=[pl.BlockSpec((1, n_e, n), lambda bi: (bi, 0, 0))],
        out_specs=[pl.BlockSpec((1, n_e, n), lambda bi: (bi, 0, 0)),
                   pl.BlockSpec((1, n_e, cap, 1), lambda bi: (bi, 0, 0, 0)),
                   pl.BlockSpec((1, n_e, cap, 1), lambda bi: (bi, 0, 0, 0)),
                   pl.BlockSpec((1, n_e, LANES), lambda bi: (bi, 0, 0))],
        out_shape=[jax.ShapeDtypeStruct((b, n_e, n), jnp.int32),
                   jax.ShapeDtypeStruct((b, n_e, cap, 1), F32),
                   jax.ShapeDtypeStruct((b, n_e, cap, 1), jnp.int32),
                   jax.ShapeDtypeStruct((b, n_e, LANES), jnp.int32)],
        compiler_params=_cparams(1),
        name="expert_choice_select",
    )(aff)


def _gather_kernel(n_tok, idx_ref, idx_next_ref, src_ref, o_ref, buf, sems):
    cap = buf.shape[1]
    e, b = pl.program_id(0), pl.program_id(1)
    n_b = pl.num_programs(1)
    step = e * n_b + b
    cur = lax.rem(step, 2)

    def row_copy(idx, sample, j, half):
        return pltpu.make_async_copy(src_ref.at[pl.ds(sample * n_tok + idx[0, 0, j], 1), :],
                                     buf.at[half, pl.ds(j, 1), :], sems.at[half])

    def start_rows(idx, sample, half):
        def issue(j, carry):
            row_copy(idx, sample, j, half).start()
            return carry
        lax.fori_loop(0, cap, issue, 0, unroll=8)

    @pl.when(step == 0)
    def _():
        start_rows(idx_ref, b, cur)

    @pl.when(step < pl.num_programs(0) * n_b - 1)
    def _():
        start_rows(idx_next_ref, lax.rem(b + 1, n_b), 1 - cur)

    def drain(j, carry):
        row_copy(idx_ref, b, j, cur).wait()
        return carry

    lax.fori_loop(0, cap, drain, 0, unroll=8)
    o_ref[0, 0] = buf[cur].astype(BF16)


def _gather(idx, b_lat, n_tok):
    cap = idx.shape[-1]
    d = b_lat.shape[-1]
    b = b_lat.shape[0] // n_tok
    n_e = idx.shape[0] // b

    def idx_row(e, bi):
        return bi * n_e + e

    def next_idx_row(e, bi):
        wrap = (bi + 1) // b
        return idx_row(jnp.minimum(e + wrap, n_e - 1), (bi + 1) % b)

    return pl.pallas_call(
        functools.partial(_gather_kernel, n_tok),
        grid=(n_e, b),
        in_specs=[pl.BlockSpec((1, 1, cap), lambda e, bi: (idx_row(e, bi), 0, 0), memory_space=pltpu.SMEM),
                  pl.BlockSpec((1, 1, cap), lambda e, bi: (next_idx_row(e, bi), 0, 0),
                               memory_space=pltpu.SMEM),
                  pl.BlockSpec(memory_space=pl.ANY)],
        out_specs=pl.BlockSpec((1, 1, cap, d), lambda e, bi: (e, bi, 0, 0)),
        out_shape=jax.ShapeDtypeStruct((n_e, b, cap, d), BF16),
        scratch_shapes=[pltpu.VMEM((2, cap, d), F32), pltpu.SemaphoreType.DMA((2,))],
        compiler_params=_cparams(2),
        name="moe_gather",
    )(idx, idx, b_lat)


def _expert_up_kernel(x_ref, wg_ref, wu_ref, o_ref, wg_scr, wu_scr):
    @pl.when(pl.program_id(2) == 0)
    def _():
        wg_scr[...] = wg_ref[0].astype(BF16)
        wu_scr[...] = wu_ref[0].astype(BF16)

    x = x_ref[0, 0]
    g = _dot(x, wg_scr[...])
    o_ref[0, 0] = (g * jax.nn.sigmoid(g) * _dot(x, wu_scr[...])).astype(BF16)


def _expert_up(xin, w_gate, w_up):
    n_e, b, cap, d = xin.shape
    f = w_gate.shape[-1]
    tn = min(f, 512)
    return pl.pallas_call(
        _expert_up_kernel,
        grid=(n_e, f // tn, b),
        in_specs=[pl.BlockSpec((1, 1, cap, d), lambda e, j, bi: (e, bi, 0, 0)),
                  pl.BlockSpec((1, d, tn), lambda e, j, bi: (e, 0, j)),
                  pl.BlockSpec((1, d, tn), lambda e, j, bi: (e, 0, j))],
        out_specs=pl.BlockSpec((1, 1, cap, tn), lambda e, j, bi: (e, bi, 0, j)),
        out_shape=jax.ShapeDtypeStruct((n_e, b, cap, f), BF16),
        scratch_shapes=[pltpu.VMEM((d, tn), BF16), pltpu.VMEM((d, tn), BF16)],
        compiler_params=_cparams(3),
        name="expert_up",
    )(xin, w_gate, w_up)


def _expert_down_kernel(h_ref, w_ref, g_ref, o_ref):
    w = w_ref[0].astype(BF16)
    for bi in range(h_ref.shape[1]):
        o_ref[0, bi] = (_dot(h_ref[0, bi], w) * g_ref[bi, 0]).astype(BF16)


def _expert_down(hid, w_down, gate):
    n_e, b, cap, f = hid.shape
    d = w_down.shape[-1]
    tn = min(d, 512)
    return pl.pallas_call(
        _expert_down_kernel,
        grid=(n_e, d // tn),
        in_specs=[pl.BlockSpec((1, b, cap, f), lambda e, j: (e, 0, 0, 0)),
                  pl.BlockSpec((1, f, tn), lambda e, j: (e, 0, j)),
                  pl.BlockSpec((b, 1, cap, 1), lambda e, j: (0, e, 0, 0))],
        out_specs=pl.BlockSpec((1, b, cap, tn), lambda e, j: (e, 0, 0, j)),
        out_shape=jax.ShapeDtypeStruct((n_e, b, cap, d), BF16),
        compiler_params=_cparams(2),
        name="expert_down",
    )(hid, w_down, gate)


def _combine_kernel(n_tiles, lo_ref, slot_ref, y_ref, h_ref, pfn_ref, g2_ref, o_ref, buf, acc, sem):
    bi, i = pl.program_id(0), pl.program_id(1)
    n_e, t = slot_ref.shape[1], slot_ref.shape[2]
    piece, chunk = MOE_PIECE_ROWS, MOE_CHUNK_ROWS
    shift = piece.bit_length() - 1

    @pl.when((bi == 0) & (i == 0))
    def _():
        buf[...] = jnp.zeros_like(buf)

    def piece_copy(e, src_piece, dst_row):
        return pltpu.make_async_copy(
            y_ref.at[e, bi, pl.ds(pl.multiple_of(src_piece * piece, piece), piece), :],
            buf.at[pl.ds(pl.multiple_of(dst_row, piece), piece), :], sem)

    filled = jnp.int32(0)
    shifts = []
    for e in range(n_e):
        row = (bi * n_e + e) * (n_tiles + 1) + i
        lo, hi = lo_ref[row], lo_ref[row + 1]
        first = lax.shift_right_logical(lo, shift)
        count = jnp.where(hi > lo, lax.shift_right_logical(hi + (piece - 1), shift) - first, 0)

        def issue(p, carry, e=e, first=first, filled=filled):
            piece_copy(e, first + p, filled + p * piece).start()
            return carry

        lax.fori_loop(0, count, issue, 0)
        shifts.append(filled - first * piece)
        filled = filled + count * piece

    def drain(p, carry):
        piece_copy(0, 0, 0).wait()
        return carry

    lax.fori_loop(0, lax.shift_right_logical(filled, shift), drain, 0)

    slot = slot_ref[0]
    target = jnp.concatenate(
        [jnp.where(slot[e:e + 1, :] >= 0, slot[e:e + 1, :] + shifts[e], -1) for e in range(n_e)], axis=0)

    acc[...] = jnp.zeros_like(acc)

    def add_chunk(k, carry):
        r0 = pl.multiple_of(k * chunk, chunk)
        rows = lax.broadcasted_iota(jnp.int32, (chunk, 1), 0) + r0
        hit = target[0:1, :] == rows
        for e in range(1, n_e):
            hit = hit | (target[e:e + 1, :] == rows)
        acc[...] += _dot_tn(hit.astype(BF16), buf[pl.ds(r0, chunk), :])
        return carry

    lax.fori_loop(0, lax.shift_right_logical(filled + (chunk - 1), chunk.bit_length() - 1), add_chunk, 0)
    o_ref[0] = h_ref[0] + g2_ref[0] * _rms(acc[...], pfn_ref[...])


def _combine(lo, slot, y, h, post_ffn, g2, tile):
    b, n_e, n = slot.shape
    cap, d = y.shape[2], y.shape[3]
    n_tiles = n // tile
    assert cap % MOE_PIECE_ROWS == 0 and MOE_CHUNK_ROWS % MOE_PIECE_ROWS == 0
    buf_rows = n_e * (tile + 2 * MOE_PIECE_ROWS)
    buf_rows = -(-buf_rows // MOE_CHUNK_ROWS) * MOE_CHUNK_ROWS
    return pl.pallas_call(
        functools.partial(_combine_kernel, n_tiles),
        grid_spec=pltpu.PrefetchScalarGridSpec(
            num_scalar_prefetch=1,
            grid=(b, n_tiles),
            in_specs=[pl.BlockSpec((1, n_e, tile), lambda bi, i, lo_ref: (bi, 0, i)),
                      pl.BlockSpec(memory_space=pl.ANY),
                      pl.BlockSpec((1, tile, d), lambda bi, i, lo_ref: (bi, i, 0)),
                      pl.BlockSpec((1, d), lambda bi, i, lo_ref: (0, 0)),
                      pl.BlockSpec((1, 1, d), lambda bi, i, lo_ref: (bi, 0, 0))],
            out_specs=pl.BlockSpec((1, tile, d), lambda bi, i, lo_ref: (bi, i, 0)),
            scratch_shapes=[pltpu.VMEM((buf_rows, d), BF16), pltpu.VMEM((tile, d), F32),
                            pltpu.SemaphoreType.DMA(())]),
        out_shape=jax.ShapeDtypeStruct((b, n, d), F32),
        compiler_params=_cparams(2),
        name="moe_combine",
    )(lo, slot, y, h, post_ffn, g2)


def _rope_tables(n):
    t = jnp.arange(n, dtype=jnp.int32)
    pos_row = (t // GRID_W).astype(F32)
    pos_col = (t % GRID_W).astype(F32)
    n_freq = HEAD_DIM // 4
    inv_freq = ROPE_THETA ** (-jnp.arange(n_freq, dtype=F32) / n_freq)
    ang_r = pos_row[:, None] * inv_freq
    ang_c = pos_col[:, None] * inv_freq
    cos = jnp.concatenate([jnp.cos(ang_r)] * 2 + [jnp.cos(ang_c)] * 2, axis=1)
    sin = jnp.concatenate([-jnp.sin(ang_r), jnp.sin(ang_r), -jnp.sin(ang_c), jnp.sin(ang_c)], axis=1)
    return cos, sin


def kernel(x, c, ctx, c_ctx, w_mod, b_mod, pre_mix_norm, post_mix_norm, pre_ffn_norm, post_ffn_norm,
           w_in, q_norm, k_norm, sgu_norm, sgu_w, sgu_b, w_br_attn, w_br_sgu, w_out,
           w_router, w_gate, w_up, w_down):
    b, n, d = x.shape
    n_ctx = ctx.shape[1]
    assert w_mod.shape[0] == 1, "single-layer block"
    n_e = w_router.shape[-1]
    cap = EC_CAPACITY_FACTOR * n // n_e

    rows = -(-(b + 1) // SUBLANES) * SUBLANES
    cc = jnp.zeros((rows, d), F32).at[:b].set(c).at[b].set(c_ctx)
    mod = _mod_vectors(cc, w_mod[0], b_mod)
    sh1, sc1, g1, sh2, sc2, g2 = [mod[:, i * d:(i + 1) * d].reshape(rows, 1, d) for i in range(N_MOD)]

    a_lat = _norm_mod(x, pre_mix_norm, sh1, sc1, lambda bi: bi).reshape(b * n, d)
    a_ctx = _norm_mod(ctx, pre_mix_norm, sh1, sc1, lambda bi: b).reshape(b * n_ctx, d)

    w_in_b = w_in[0].astype(BF16)
    cos, sin = _rope_tables(n)
    q, k, v, u, s, ga, gb = _in_proj(a_lat, w_in_b, cos, sin, q_norm, k_norm, sgu_norm, n, d)
    k_c, v_c = _ctx_kv(a_ctx, w_in_b, k_norm, d)

    attn = _attention(q.reshape(b, n, ATTN_WIDTH), k.reshape(b, n, KV_WIDTH), v.reshape(b, n, KV_WIDTH),
                      k_c.reshape(b, n_ctx, KV_WIDTH), v_c.reshape(b, n_ctx, KV_WIDTH))
    merged = _merge(attn.reshape(b * n, ATTN_WIDTH), u, s, sgu_w[0], sgu_b[0].T,
                    w_br_attn[0].astype(BF16), w_br_sgu[0].astype(BF16), ga, gb, n, d)
    h1, b_lat = _out_proj(merged, w_out[0].astype(BF16), x.reshape(b * n, d), post_mix_norm, g1,
                          pre_ffn_norm, sh2, sc2, n, d)
    h1 = h1.reshape(b, n, d)

    tile = min(n, MOE_TOKEN_TILE)
    aff = _router(b_lat.reshape(b, n, d), w_router[0].T)
    slot, gate, idx, lo = _select(aff, cap, tile)
    xin = _gather(idx.reshape(b * n_e, 1, cap), b_lat, n)
    hid = _expert_up(xin, w_gate[0], w_up[0])
    y = _expert_down(hid, w_down[0], gate)
    lo_flat = lo[:, :, :n // tile + 1].reshape(-1)
    return _combine(lo_flat, slot, y, h1, post_ffn_norm, g2, tile)
```

```python
import functools
import math

import jax
import jax.numpy as jnp
from jax import lax
from jax.experimental import pallas as pl
from jax.experimental.pallas import tpu as pltpu

F32 = jnp.float32
BF16 = jnp.bfloat16

GRID_W = 64
N_HEADS = 16
N_KV_HEADS = 4
HEAD_DIM = 128
GQA_GROUP = N_HEADS // N_KV_HEADS
ATTN_WIDTH = N_HEADS * HEAD_DIM
KV_WIDTH = N_KV_HEADS * HEAD_DIM
ROPE_THETA = 10000.0
SGU_CHUNK = 128
SGU_GROUPS = 8
EC_CAPACITY_FACTOR = 2
NORM_EPS = 1e-6
N_MOD = 6
QK_PRESCALE = HEAD_DIM ** -0.5 * math.log2(math.e)
ATTN_KEY_CHUNK = 512
MOE_PIECE_ROWS = 16
MOE_CHUNK_ROWS = 256
MOE_TOKEN_TILE = 128
MOE_COMBINE_GROUP = 8

LANES = 128
SUBLANES = 8
VMEM_BUDGET_BYTES = 56 * 1024 * 1024


def _cparams(n_axes, vmem=VMEM_BUDGET_BYTES):
    return pltpu.CompilerParams(
        dimension_semantics=("arbitrary",) * n_axes, vmem_limit_bytes=vmem)


def _rms(x, gain):
    return x * lax.rsqrt(jnp.mean(x * x, axis=-1, keepdims=True) + NORM_EPS) * gain


def _dot(a, b):
    return jnp.dot(a, b, preferred_element_type=F32)


def _dot_nt(a, b):
    return lax.dot_general(a, b, (((1,), (1,)), ((), ())), preferred_element_type=F32)


def _dot_tn(a, b):
    return lax.dot_general(a, b, (((0,), (0,)), ((), ())), preferred_element_type=F32)


def _mod_kernel(c_ref, w_ref, b_ref, o_ref):
    c = c_ref[...]
    s = (c * jax.nn.sigmoid(c)).astype(BF16)
    o_ref[...] = _dot(s, w_ref[...].astype(BF16)) + b_ref[...]


def _mod_vectors(cc, w_mod, b_mod):
    rows, d = cc.shape
    n = w_mod.shape[1]
    tn = min(n, 512)
    return pl.pallas_call(
        _mod_kernel,
        grid=(n // tn,),
        in_specs=[pl.BlockSpec((rows, d), lambda j: (0, 0)),
                  pl.BlockSpec((d, tn), lambda j: (0, j)),
                  pl.BlockSpec((1, tn), lambda j: (0, j))],
        out_specs=pl.BlockSpec((rows, tn), lambda j: (0, j)),
        out_shape=jax.ShapeDtypeStruct((rows, n), F32),
        compiler_params=_cparams(1),
        name="mod_vectors",
    )(cc, w_mod, b_mod)


def _norm_mod_kernel(x_ref, g_ref, sh_ref, sc_ref, o_ref):
    x = x_ref[0]
    o_ref[0] = (_rms(x, g_ref[...]) * (1.0 + sc_ref[0]) + sh_ref[0]).astype(BF16)


def _norm_mod(x, gain, shift, scale, row_of_batch):
    b, n, d = x.shape
    tm = min(n, 512)
    return pl.pallas_call(
        _norm_mod_kernel,
        grid=(b, n // tm),
        in_specs=[pl.BlockSpec((1, tm, d), lambda bi, i: (bi, i, 0)),
                  pl.BlockSpec((1, d), lambda bi, i: (0, 0)),
                  pl.BlockSpec((1, 1, d), lambda bi, i: (row_of_batch(bi), 0, 0)),
                  pl.BlockSpec((1, 1, d), lambda bi, i: (row_of_batch(bi), 0, 0))],
        out_specs=pl.BlockSpec((1, tm, d), lambda bi, i: (bi, i, 0)),
        out_shape=jax.ShapeDtypeStruct((b, n, d), BF16),
        compiler_params=_cparams(2),
        name="norm_modulate",
    )(x, gain, shift, scale)


def _head_norm_rope(acc, gain, cos, sin, lane_lo):
    outs = []
    for h in range(acc.shape[1] // HEAD_DIM):
        y = _rms(acc[:, h * HEAD_DIM:(h + 1) * HEAD_DIM], gain)
        if cos is not None:
            partner = jnp.where(lane_lo, pltpu.roll(y, HEAD_DIM - HEAD_DIM // 4, axis=1),
                                pltpu.roll(y, HEAD_DIM // 4, axis=1))
            y = y * cos + partner * sin
        outs.append(y)
    return outs[0] if len(outs) == 1 else jnp.concatenate(outs, axis=1)


def _rope_lane_lo():
    lane = lax.broadcasted_iota(jnp.int32, (1, HEAD_DIM), 1)
    return (lane % (HEAD_DIM // 2)) < (HEAD_DIM // 4)


def _in_proj_kernel(kv_tile, s_tiles, group_dim, a_ref, w_ref, cos_ref, sin_ref, kn_ref, sn_ref, o_ref):
    j = pl.program_id(1)
    acc = _dot(a_ref[...], w_ref[...])
    s_lo, s_hi = s_tiles
    plain = (j != kv_tile) & ((j < s_lo) | (j >= s_hi))

    @pl.when(plain)
    def _():
        o_ref[...] = acc.astype(BF16)

    @pl.when(j == kv_tile)
    def _():
        k = _head_norm_rope(acc[:, :KV_WIDTH], kn_ref[...], cos_ref[...], sin_ref[...], _rope_lane_lo())
        o_ref[:, :KV_WIDTH] = k.astype(BF16)
        o_ref[:, KV_WIDTH:] = acc[:, KV_WIDTH:].astype(BF16)

    @pl.when((j >= s_lo) & (j < s_hi))
    def _():
        gain = sn_ref[...]
        for c in range(acc.shape[1] // group_dim):
            cols = slice(c * group_dim, (c + 1) * group_dim)
            o_ref[:, cols] = _rms(jax.nn.gelu(acc[:, cols]), gain[:, cols]).astype(BF16)


def _proj_layout(d):
    sw = d // 2
    names = ("q", "k", "v", "u", "s", "ga", "gb")
    widths = (ATTN_WIDTH, KV_WIDTH, KV_WIDTH, sw, sw, d, d)
    off, out = 0, {}
    for name, wd in zip(names, widths):
        out[name] = off
        off += wd
    out["end"] = off
    return out


def _in_proj(a, w, cos, sin, k_norm, sgu_norm, seq, d):
    m = a.shape[0]
    sw = d // 2
    tm = min(seq, 1024)
    tn = 2 * KV_WIDTH
    lay = _proj_layout(d)
    assert all(off % tn == 0 for name, off in lay.items() if name != "v") and lay["v"] == lay["k"] + KV_WIDTH
    s_lo = lay["s"] // tn
    s_tiles = (s_lo, lay["ga"] // tn)
    pos_tiles = seq // tm
    return pl.pallas_call(
        functools.partial(_in_proj_kernel, lay["k"] // tn, s_tiles, sw // SGU_GROUPS),
        grid=(m // tm, lay["end"] // tn),
        in_specs=[pl.BlockSpec((tm, d), lambda i, j: (i, 0)),
                  pl.BlockSpec((d, tn), lambda i, j: (0, j)),
                  pl.BlockSpec((tm, HEAD_DIM), lambda i, j: (i % pos_tiles, 0)),
                  pl.BlockSpec((tm, HEAD_DIM), lambda i, j: (i % pos_tiles, 0)),
                  pl.BlockSpec((1, HEAD_DIM), lambda i, j: (0, 0)),
                  pl.BlockSpec((1, tn), lambda i, j: (0, jnp.clip(j - s_lo, 0, s_tiles[1] - s_lo - 1)))],
        out_specs=pl.BlockSpec((tm, tn), lambda i, j: (i, j)),
        out_shape=jax.ShapeDtypeStruct((m, lay["end"]), BF16),
        compiler_params=_cparams(2),
        name="in_projection",
    )(a, w, cos, sin, k_norm, sgu_norm)


def _ctx_kv_kernel(a_ref, wk_ref, wv_ref, kn_ref, k_ref, v_ref):
    a = a_ref[...]
    k_ref[...] = _head_norm_rope(_dot(a, wk_ref[...]), kn_ref[...], None, None, None).astype(BF16)
    v_ref[...] = _dot(a, wv_ref[...]).astype(BF16)


def _ctx_kv(a, w, k_norm, d):
    m = a.shape[0]
    tm = min(m, 512)
    kb = ATTN_WIDTH // KV_WIDTH
    return pl.pallas_call(
        _ctx_kv_kernel,
        grid=(m // tm,),
        in_specs=[pl.BlockSpec((tm, d), lambda i: (i, 0)),
                  pl.BlockSpec((d, KV_WIDTH), lambda i: (0, kb)),
                  pl.BlockSpec((d, KV_WIDTH), lambda i: (0, kb + 1)),
                  pl.BlockSpec((1, HEAD_DIM), lambda i: (0, 0))],
        out_specs=[pl.BlockSpec((tm, KV_WIDTH), lambda i: (i, 0))] * 2,
        out_shape=[jax.ShapeDtypeStruct((m, KV_WIDTH), BF16)] * 2,
        compiler_params=_cparams(1),
        name="context_kv",
    )(a, w, w, k_norm)


def _lane_fold(x, op):
    out = x[:, :LANES]
    for c in range(1, x.shape[1] // LANES):
        out = op(out, x[:, c * LANES:(c + 1) * LANES])
    return out


def _attn_kernel(q_ref, kl_ref, vl_ref, kc_ref, vc_ref, qn_ref, cos_ref, sin_ref, o_ref, s_scr):
    tq = q_ref.shape[1]
    n, n_ctx = kl_ref.shape[1], kc_ref.shape[1]
    ck = min(n, ATTN_KEY_CHUNK)
    chunks = [(kl_ref, vl_ref, c * ck, ck, c * ck) for c in range(n // ck)] + [(kc_ref, vc_ref, 0, n_ctx, n)]
    lane_lo = _rope_lane_lo()

    def scores(g):
        q = q_ref[0, :, g * HEAD_DIM:(g + 1) * HEAD_DIM].astype(F32)
        q = _head_norm_rope(q, qn_ref[...], cos_ref[...], sin_ref[...], lane_lo)
        q = (q * QK_PRESCALE).astype(BF16)
        mvec = None
        for k_ref, _, r0, w, col in chunks:
            s = _dot_nt(q, k_ref[0, r0:r0 + w, :])
            s_scr[g, :, col:col + w] = s
            part = _lane_fold(s, jnp.maximum)
            mvec = part if mvec is None else jnp.maximum(mvec, part)
        return jnp.max(mvec, axis=1, keepdims=True)

    def weighted_values(g, m):
        lvec = jnp.zeros((tq, LANES), F32)
        acc = jnp.zeros((tq, HEAD_DIM), F32)
        for _, v_ref, r0, w, col in chunks:
            p = jnp.exp2(s_scr[g, :, col:col + w] - m)
            lvec = lvec + _lane_fold(p, jnp.add)
            acc = acc + _dot(p.astype(BF16), v_ref[0, r0:r0 + w, :])
        o = acc / jnp.sum(lvec, axis=1, keepdims=True)
        o_ref[0, :, g * HEAD_DIM:(g + 1) * HEAD_DIM] = o.astype(BF16)

    m_prev = None
    for g in range(GQA_GROUP + 1):
        m_cur = scores(g) if g < GQA_GROUP else None
        if g > 0:
            weighted_values(g - 1, m_prev)
        m_prev = m_cur


def _attention(proj, k_ctx, v_ctx, q_norm, cos, sin, lay):
    b, n, _ = proj.shape
    n_ctx = k_ctx.shape[1]
    tq = min(n, 256)
    qw = GQA_GROUP * HEAD_DIM
    assert n % min(n, ATTN_KEY_CHUNK) == 0
    q0, k0, v0 = lay["q"] // qw, lay["k"] // HEAD_DIM, lay["v"] // HEAD_DIM
    return pl.pallas_call(
        _attn_kernel,
        grid=(b, N_KV_HEADS, n // tq),
        in_specs=[pl.BlockSpec((1, tq, qw), lambda bi, h, i: (bi, i, q0 + h)),
                  pl.BlockSpec((1, n, HEAD_DIM), lambda bi, h, i: (bi, 0, k0 + h)),
                  pl.BlockSpec((1, n, HEAD_DIM), lambda bi, h, i: (bi, 0, v0 + h)),
                  pl.BlockSpec((1, n_ctx, HEAD_DIM), lambda bi, h, i: (bi, 0, h)),
                  pl.BlockSpec((1, n_ctx, HEAD_DIM), lambda bi, h, i: (bi, 0, h)),
                  pl.BlockSpec((1, HEAD_DIM), lambda bi, h, i: (0, 0)),
                  pl.BlockSpec((tq, HEAD_DIM), lambda bi, h, i: (i, 0)),
                  pl.BlockSpec((tq, HEAD_DIM), lambda bi, h, i: (i, 0))],
        out_specs=pl.BlockSpec((1, tq, qw), lambda bi, h, i: (bi, i, h)),
        out_shape=jax.ShapeDtypeStruct((b, n, ATTN_WIDTH), BF16),
        scratch_shapes=[pltpu.VMEM((GQA_GROUP, tq, n + n_ctx), F32)],
        compiler_params=_cparams(3),
        name="gqa_attention",
    )(proj, proj, proj, k_ctx, v_ctx, q_norm, cos, sin)


def _merge_kernel(group_dim, attn_ref, u0_ref, u1_ref, s0_ref, s1_ref, sw_ref, sbt_ref, wa_ref, ws_ref,
                  ga_ref, gb_ref, o_ref, sgu_scr):
    j = pl.program_id(1)
    tm = attn_ref.shape[0]
    half = SGU_GROUPS // 2

    @pl.when(j == 0)
    def _():
        for g in range(SGU_GROUPS):
            u_ref, s_ref = (u0_ref, s0_ref) if g < half else (u1_ref, s1_ref)
            wg = sw_ref[g].astype(BF16)
            bias = sbt_ref[:, g:g + 1]
            cols = slice((g % half) * group_dim, (g % half + 1) * group_dim)
            out_cols = slice(g * group_dim, (g + 1) * group_dim)
            for c in range(tm // SGU_CHUNK):
                rows = slice(c * SGU_CHUNK, (c + 1) * SGU_CHUNK)
                mixed = _dot(wg, s_ref[rows, cols]) + bias
                sgu_scr[rows, out_cols] = (jax.nn.gelu(u_ref[rows, cols].astype(F32)) * mixed).astype(BF16)

    ya = _dot(attn_ref[...], wa_ref[...])
    ys = _dot(sgu_scr[...], ws_ref[...])
    gate_a = jax.nn.sigmoid(ga_ref[...].astype(F32))
    gate_b = jax.nn.sigmoid(gb_ref[...].astype(F32))
    o_ref[...] = (gate_a * ya + gate_b * ys).astype(BF16)


def _merge(attn, proj, sgu_w, sgu_bt, w_br_attn, w_br_sgu, seq, d, lay):
    m = attn.shape[0]
    sw = d // 2
    hw = sw // 2
    tm = min(seq, 512)
    tn = 512
    assert lay["u"] % hw == 0 and lay["s"] % hw == 0 and lay["ga"] % tn == 0 and lay["gb"] % tn == 0
    u0, s0, ga0, gb0 = lay["u"] // hw, lay["s"] // hw, lay["ga"] // tn, lay["gb"] // tn
    return pl.pallas_call(
        functools.partial(_merge_kernel, sw // SGU_GROUPS),
        grid=(m // tm, d // tn),
        in_specs=[pl.BlockSpec((tm, ATTN_WIDTH), lambda i, j: (i, 0)),
                  pl.BlockSpec((tm, hw), lambda i, j: (i, u0)),
                  pl.BlockSpec((tm, hw), lambda i, j: (i, u0 + 1)),
                  pl.BlockSpec((tm, hw), lambda i, j: (i, s0)),
                  pl.BlockSpec((tm, hw), lambda i, j: (i, s0 + 1)),
                  pl.BlockSpec((SGU_GROUPS, SGU_CHUNK, SGU_CHUNK), lambda i, j: (0, 0, 0)),
                  pl.BlockSpec((SGU_CHUNK, SGU_GROUPS), lambda i, j: (0, 0)),
                  pl.BlockSpec((ATTN_WIDTH, tn), lambda i, j: (0, j)),
                  pl.BlockSpec((sw, tn), lambda i, j: (0, j)),
                  pl.BlockSpec((tm, tn), lambda i, j: (i, ga0 + j)),
                  pl.BlockSpec((tm, tn), lambda i, j: (i, gb0 + j))],
        out_specs=pl.BlockSpec((tm, tn), lambda i, j: (i, j)),
        out_shape=jax.ShapeDtypeStruct((m, d), BF16),
        scratch_shapes=[pltpu.VMEM((tm, sw), BF16)],
        compiler_params=_cparams(2),
        name="sgu_branch_merge",
    )(attn, proj, proj, proj, proj, sgu_w, sgu_bt, w_br_attn, w_br_sgu, proj, proj)


def _out_proj_kernel(m_ref, w_ref, x_hbm, pmn_ref, g1_ref, pfn_ref, sh2_ref, sc2_ref,
                     h_hbm, b_hbm, y_scr, x_buf, h_buf, b_buf, sems):
    i, j = pl.program_id(0), pl.program_id(1)
    last_i, last_j = pl.num_programs(0) - 1, pl.num_programs(1) - 1
    tm = m_ref.shape[0]

    def rows(t):
        return pl.ds(pl.multiple_of(t * tm, tm), tm)

    def x_read():
        return pltpu.make_async_copy(x_hbm.at[rows(i), :], x_buf, sems.at[0])

    def h_write(t):
        return pltpu.make_async_copy(h_buf, h_hbm.at[rows(t), :], sems.at[1])

    def b_write(t):
        return pltpu.make_async_copy(b_buf, b_hbm.at[rows(t), :], sems.at[2])

    @pl.when(j == 0)
    def _():
        x_read().start()

    y_scr[j] = _dot(m_ref[...], w_ref[...])

    @pl.when(j == last_j)
    def _():
        x_read().wait()

        @pl.when(i > 0)
        def _():
            h_write(i - 1).wait()
            b_write(i - 1).wait()

        n_t, _, tn = y_scr.shape
        d = n_t * tn
        col = lambda t: slice(t * tn, (t + 1) * tn)
        ssq = sum(jnp.sum(y_scr[t] * y_scr[t], axis=1, keepdims=True) for t in range(n_t))
        y_inv = lax.rsqrt(ssq / d + NORM_EPS)
        ssq = jnp.zeros_like(ssq)
        for t in range(n_t):
            h = x_buf[:, col(t)] + g1_ref[0, :, col(t)] * (y_scr[t] * y_inv * pmn_ref[:, col(t)])
            h_buf[:, col(t)] = h
            ssq = ssq + jnp.sum(h * h, axis=1, keepdims=True)
        h_inv = lax.rsqrt(ssq / d + NORM_EPS)
        for t in range(n_t):
            b_buf[:, col(t)] = (h_buf[:, col(t)] * h_inv * pfn_ref[:, col(t)] * (1.0 + sc2_ref[0, :, col(t)])
                                + sh2_ref[0, :, col(t)])
        h_write(i).start()
        b_write(i).start()

        @pl.when(i == last_i)
        def _():
            h_write(i).wait()
            b_write(i).wait()


def _out_proj(merged, w_out, x, post_mix, g1, pre_ffn, sh2, sc2, seq, d):
    m = merged.shape[0]
    tm = min(seq, 512)
    tn = 512
    tiles_per_batch = seq // tm
    row = lambda i, j: (i // tiles_per_batch, 0, 0)
    any_spec = pl.BlockSpec(memory_space=pl.ANY)
    return pl.pallas_call(
        _out_proj_kernel,
        grid=(m // tm, d // tn),
        in_specs=[pl.BlockSpec((tm, d), lambda i, j: (i, 0)),
                  pl.BlockSpec((d, tn), lambda i, j: (0, j)),
                  any_spec,
                  pl.BlockSpec((1, d), lambda i, j: (0, 0)),
                  pl.BlockSpec((1, 1, d), row),
                  pl.BlockSpec((1, d), lambda i, j: (0, 0)),
                  pl.BlockSpec((1, 1, d), row),
                  pl.BlockSpec((1, 1, d), row)],
        out_specs=[any_spec, any_spec],
        out_shape=[jax.ShapeDtypeStruct((m, d), F32), jax.ShapeDtypeStruct((m, d), F32)],
        scratch_shapes=[pltpu.VMEM((d // tn, tm, tn), F32), pltpu.VMEM((tm, d), F32),
                        pltpu.VMEM((tm, d), F32), pltpu.VMEM((tm, d), F32),
                        pltpu.SemaphoreType.DMA((3,))],
        compiler_params=_cparams(2),
        name="out_projection",
    )(merged, w_out, x, post_mix, g1, pre_ffn, sh2, sc2)


def _split_bf16(x):
    hi = x.astype(BF16)
    return hi, (x - hi.astype(F32)).astype(BF16)


def _router_kernel(x_ref, wr_ref, aff_ref):
    n_e = wr_ref.shape[0]
    b_hi, b_lo = _split_bf16(x_ref[0])
    w_hi, w_lo = _split_bf16(wr_ref[...])
    p1 = _dot_nt(jnp.concatenate([w_hi, w_lo], axis=0), b_hi)
    logits = p1[:n_e] + p1[n_e:] + _dot_nt(w_hi, b_lo)
    z = jnp.exp(logits - jnp.max(logits, axis=0, keepdims=True))
    aff_ref[0] = z / jnp.sum(z, axis=0, keepdims=True)


def _router(b_lat, w_router_t):
    b, n, d = b_lat.shape
    n_e = w_router_t.shape[0]
    t = min(n, 512)
    return pl.pallas_call(
        _router_kernel,
        grid=(b, n // t),
        in_specs=[pl.BlockSpec((1, t, d), lambda bi, i: (bi, i, 0)),
                  pl.BlockSpec((n_e, d), lambda bi, i: (0, 0))],
        out_specs=pl.BlockSpec((1, n_e, t), lambda bi, i: (bi, 0, i)),
        out_shape=jax.ShapeDtypeStruct((b, n_e, n), F32),
        compiler_params=_cparams(2),
        name="router_affinity",
    )(b_lat, w_router_t)


def _prefix_count(x01, tri):
    rows, n = x01.shape
    off = jnp.zeros((rows, 1), F32)
    outs = []
    for c in range(n // LANES):
        y = _dot(x01[:, c * LANES:(c + 1) * LANES].astype(BF16), tri) + off
        outs.append(y)
        off = y[:, LANES - 1:LANES]
    return jnp.concatenate(outs, axis=1)


def _select_kernel(cap, tile_shift, aff_ref, slot_ref, gate_ref, idx_ref, lo_ref):
    aff = aff_ref[0]
    n_e, n = aff.shape
    bits = pltpu.bitcast(aff, jnp.int32)
    r = lax.broadcasted_iota(jnp.int32, (LANES, LANES), 0)
    c = lax.broadcasted_iota(jnp.int32, (LANES, LANES), 1)
    tri = (r <= c).astype(BF16)
    strict = (r < c).astype(BF16)

    def body(i, ans):
        cand = ans | lax.shift_left(jnp.int32(1), 30 - i)
        cnt = jnp.sum((bits >= cand).astype(F32), axis=1, keepdims=True)
        return jnp.where(cnt >= cap, cand, ans)

    kth = lax.fori_loop(0, 31, body, jnp.zeros((n_e, 1), jnp.int32))
    above = bits > kth
    tied = bits == kth
    need = cap - jnp.sum(above.astype(F32), axis=1, keepdims=True)
    tied_rank = _prefix_count(tied.astype(F32), tri)
    chosen = above | (tied & (tied_rank <= need))
    slot = _prefix_count(chosen.astype(F32), tri) - 1.0
    slot = jnp.where(chosen, slot, -1.0).astype(jnp.int32)
    slot_ref[0] = slot

    j = lax.broadcasted_iota(jnp.int32, (cap, 1), 0)
    tok = lax.broadcasted_iota(jnp.int32, (1, n), 1).astype(F32)
    for e in range(n_e):
        hit = slot[e:e + 1, :] == j
        gate_ref[0, e] = jnp.sum(jnp.where(hit, aff[e:e + 1, :], 0.0), axis=1, keepdims=True)
        idx_ref[0, e] = jnp.sum(jnp.where(hit, tok, 0.0), axis=1, keepdims=True).astype(jnp.int32)

    t_of = lax.shift_right_logical(lax.broadcasted_iota(jnp.int32, (n, LANES), 0), tile_shift)
    in_tile = (t_of == lax.broadcasted_iota(jnp.int32, (n, LANES), 1)).astype(BF16)
    per_tile = _dot(chosen.astype(BF16), in_tile)
    lo_ref[0] = _dot(per_tile.astype(BF16), strict).astype(jnp.int32)


def _select(aff, cap, tile):
    b, n_e, n = aff.shape
    tile_shift = tile.bit_length() - 1
    assert tile == 1 << tile_shift and tile <= 256 and n // tile < LANES
    return pl.pallas_call(
        functools.partial(_select_kernel, cap, tile_shift),
        grid=(b,),
        in_specs=[pl.BlockSpec((1, n_e, n), lambda bi: (bi, 0, 0))],
        out_specs=[pl.BlockSpec((1, n_e, n), lambda bi: (bi, 0, 0)),
                   pl.BlockSpec((1, n_e, cap, 1), lambda bi: (bi, 0, 0, 0)),
                   pl.BlockSpec((1, n_e, cap, 1), lambda bi: (bi, 0, 0, 0)),
                   pl.BlockSpec((1, n_e, LANES), lambda bi: (bi, 0, 0))],
        out_shape=[jax.ShapeDtypeStruct((b, n_e, n), jnp.int32),
                   jax.ShapeDtypeStruct((b, n_e, cap, 1), F32),
                   jax.ShapeDtypeStruct((b, n_e, cap, 1), jnp.int32),
                   jax.ShapeDtypeStruct((b, n_e, LANES), jnp.int32)],
        compiler_params=_cparams(1),
        name="expert_choice_select",
    )(aff)


def _gather_kernel(n_tok, idx_ref, idx_next_ref, src_ref, o_ref, buf, sems):
    cap = buf.shape[1]
    e, b = pl.program_id(0), pl.program_id(1)
    n_b = pl.num_programs(1)
    step = e * n_b + b
    cur = lax.rem(step, 2)

    def row_copy(idx, sample, j, half):
        return pltpu.make_async_copy(src_ref.at[pl.ds(sample * n_tok + idx[0, 0, j], 1), :],
                                     buf.at[half, pl.ds(j, 1), :], sems.at[half])

    def start_rows(idx, sample, half):
        def issue(j, carry):
            row_copy(idx, sample, j, half).start()
            return carry
        lax.fori_loop(0, cap, issue, 0, unroll=8)

    @pl.when(step == 0)
    def _():
        start_rows(idx_ref, b, cur)

    @pl.when(step < pl.num_programs(0) * n_b - 1)
    def _():
        start_rows(idx_next_ref, lax.rem(b + 1, n_b), 1 - cur)

    def drain(j, carry):
        row_copy(idx_ref, b, j, cur).wait()
        return carry

    lax.fori_loop(0, cap, drain, 0, unroll=8)
    o_ref[0, 0] = buf[cur].astype(BF16)


def _gather(idx, b_lat, n_tok):
    cap = idx.shape[-1]
    d = b_lat.shape[-1]
    b = b_lat.shape[0] // n_tok
    n_e = idx.shape[0] // b

    def idx_row(e, bi):
        return bi * n_e + e

    def next_idx_row(e, bi):
        wrap = (bi + 1) // b
        return idx_row(jnp.minimum(e + wrap, n_e - 1), (bi + 1) % b)

    return pl.pallas_call(
        functools.partial(_gather_kernel, n_tok),
        grid=(n_e, b),
        in_specs=[pl.BlockSpec((1, 1, cap), lambda e, bi: (idx_row(e, bi), 0, 0), memory_space=pltpu.SMEM),
                  pl.BlockSpec((1, 1, cap), lambda e, bi: (next_idx_row(e, bi), 0, 0),
                               memory_space=pltpu.SMEM),
                  pl.BlockSpec(memory_space=pl.ANY)],
        out_specs=pl.BlockSpec((1, 1, cap, d), lambda e, bi: (e, bi, 0, 0)),
        out_shape=jax.ShapeDtypeStruct((n_e, b, cap, d), BF16),
        scratch_shapes=[pltpu.VMEM((2, cap, d), F32), pltpu.SemaphoreType.DMA((2,))],
        compiler_params=_cparams(2),
        name="moe_gather",
    )(idx, idx, b_lat)


def _expert_up_kernel(x_ref, wg_ref, wu_ref, o_ref):
    wg = wg_ref[0].astype(BF16)
    wu = wu_ref[0].astype(BF16)
    for bi in range(x_ref.shape[1]):
        x = x_ref[0, bi]
        g = _dot(x, wg)
        o_ref[0, bi] = (g * jax.nn.sigmoid(g) * _dot(x, wu)).astype(BF16)


def _expert_up(xin, w_gate, w_up):
    n_e, b, cap, d = xin.shape
    f = w_gate.shape[-1]
    tn = min(f, 256)
    return pl.pallas_call(
        _expert_up_kernel,
        grid=(n_e, f // tn),
        in_specs=[pl.BlockSpec((1, b, cap, d), lambda e, j: (e, 0, 0, 0), pipeline_mode=pl.Buffered(1)),
                  pl.BlockSpec((1, d, tn), lambda e, j: (e, 0, j)),
                  pl.BlockSpec((1, d, tn), lambda e, j: (e, 0, j))],
        out_specs=pl.BlockSpec((1, b, cap, tn), lambda e, j: (e, 0, 0, j)),
        out_shape=jax.ShapeDtypeStruct((n_e, b, cap, f), BF16),
        compiler_params=_cparams(2),
        name="expert_up",
    )(xin, w_gate, w_up)


def _expert_down_kernel(h_ref, w_ref, g_ref, o_ref):
    w = w_ref[0].astype(BF16)
    for bi in range(h_ref.shape[1]):
        o_ref[0, bi] = (_dot(h_ref[0, bi], w) * g_ref[bi, 0]).astype(BF16)


def _expert_down(hid, w_down, gate):
    n_e, b, cap, f = hid.shape
    d = w_down.shape[-1]
    tn = min(d, 512)
    return pl.pallas_call(
        _expert_down_kernel,
        grid=(n_e, d // tn),
        in_specs=[pl.BlockSpec((1, b, cap, f), lambda e, j: (e, 0, 0, 0)),
                  pl.BlockSpec((1, f, tn), lambda e, j: (e, 0, j)),
                  pl.BlockSpec((b, 1, cap, 1), lambda e, j: (0, e, 0, 0))],
        out_specs=pl.BlockSpec((1, b, cap, tn), lambda e, j: (e, 0, 0, j)),
        out_shape=jax.ShapeDtypeStruct((n_e, b, cap, d), BF16),
        compiler_params=_cparams(2),
        name="expert_down",
    )(hid, w_down, gate)


def _combine_kernel(n_e, n_tiles, lo_ref, slot_ref, y_ref, h_ref, pfn_ref, g2_ref, o_ref, buf, acc, sems):
    bi, i, grp = pl.program_id(0), pl.program_id(1), pl.program_id(2)
    n_groups = pl.num_programs(2)
    group = slot_ref.shape[1]
    piece, chunk = MOE_PIECE_ROWS, MOE_CHUNK_ROWS
    shift = piece.bit_length() - 1
    step = (bi * n_tiles + i) * n_groups + grp
    cur = lax.rem(step, 2)

    def piece_ranges(sample, tile, g):
        out = []
        for k in range(group):
            e = g * group + k
            row = (sample * n_e + e) * (n_tiles + 1) + tile
            lo, hi = lo_ref[row], lo_ref[row + 1]
            first = lax.shift_right_logical(lo, shift)
            count = jnp.where(hi > lo, lax.shift_right_logical(hi + (piece - 1), shift) - first, 0)
            out.append((e, first, count))
        return out

    def piece_copy(sample, e, src_piece, half, dst_row):
        return pltpu.make_async_copy(
            y_ref.at[e, sample, pl.ds(pl.multiple_of(src_piece * piece, piece), piece), :],
            buf.at[half, pl.ds(pl.multiple_of(dst_row, piece), piece), :], sems.at[half])

    def start_fetch(sample, tile, g, half):
        filled = jnp.int32(0)
        for e, first, count in piece_ranges(sample, tile, g):
            def issue(p, carry, e=e, first=first, filled=filled):
                piece_copy(sample, e, first + p, half, filled + p * piece).start()
                return carry
            lax.fori_loop(0, count, issue, 0)
            filled = filled + count * piece

    @pl.when(step == 0)
    def _():
        buf[...] = jnp.zeros_like(buf)
        start_fetch(bi, i, grp, cur)

    @pl.when(step < pl.num_programs(0) * n_tiles * n_groups - 1)
    def _():
        wrap_g = grp == n_groups - 1
        nxt_i = jnp.where(wrap_g, i + 1, i)
        wrap_i = nxt_i == n_tiles
        start_fetch(jnp.where(wrap_i, bi + 1, bi), jnp.where(wrap_i, 0, nxt_i),
                    jnp.where(wrap_g, 0, grp + 1), 1 - cur)

    filled = jnp.int32(0)
    shifts = []
    for _, first, count in piece_ranges(bi, i, grp):
        shifts.append(filled - first * piece)
        filled = filled + count * piece

    def drain(p, carry):
        piece_copy(bi, 0, 0, cur, 0).wait()
        return carry

    lax.fori_loop(0, lax.shift_right_logical(filled, shift), drain, 0)

    slot = slot_ref[0]
    target = jnp.concatenate(
        [jnp.where(slot[k:k + 1, :] >= 0, slot[k:k + 1, :] + shifts[k], -1) for k in range(group)], axis=0)

    @pl.when(grp == 0)
    def _():
        acc[...] = jnp.zeros_like(acc)

    def add_chunk(c, carry):
        r0 = pl.multiple_of(c * chunk, chunk)
        rows = lax.broadcasted_iota(jnp.int32, (chunk, 1), 0) + r0
        hit = target[0:1, :] == rows
        for k in range(1, group):
            hit = hit | (target[k:k + 1, :] == rows)
        acc[...] += _dot_tn(hit.astype(BF16), buf[cur, pl.ds(r0, chunk), :])
        return carry

    lax.fori_loop(0, lax.shift_right_logical(filled + (chunk - 1), chunk.bit_length() - 1), add_chunk, 0)

    @pl.when(grp == n_groups - 1)
    def _():
        o_ref[0] = h_ref[0] + g2_ref[0] * _rms(acc[...], pfn_ref[...])


def _combine(lo, slot, y, h, post_ffn, g2, tile):
    b, n_e, n = slot.shape
    cap, d = y.shape[2], y.shape[3]
    n_tiles = n // tile
    group = MOE_COMBINE_GROUP
    assert cap % MOE_PIECE_ROWS == 0 and MOE_CHUNK_ROWS % MOE_PIECE_ROWS == 0 and n_e % group == 0
    buf_rows = group * (tile + 2 * MOE_PIECE_ROWS)
    buf_rows = -(-buf_rows // MOE_CHUNK_ROWS) * MOE_CHUNK_ROWS
    return pl.pallas_call(
        functools.partial(_combine_kernel, n_e, n_tiles),
        grid_spec=pltpu.PrefetchScalarGridSpec(
            num_scalar_prefetch=1,
            grid=(b, n_tiles, n_e // group),
            in_specs=[pl.BlockSpec((1, group, tile), lambda bi, i, g, lo_ref: (bi, g, i)),
                      pl.BlockSpec(memory_space=pl.ANY),
                      pl.BlockSpec((1, tile, d), lambda bi, i, g, lo_ref: (bi, i, 0)),
                      pl.BlockSpec((1, d), lambda bi, i, g, lo_ref: (0, 0)),
                      pl.BlockSpec((1, 1, d), lambda bi, i, g, lo_ref: (bi, 0, 0))],
            out_specs=pl.BlockSpec((1, tile, d), lambda bi, i, g, lo_ref: (bi, i, 0)),
            scratch_shapes=[pltpu.VMEM((2, buf_rows, d), BF16), pltpu.VMEM((tile, d), F32),
                            pltpu.SemaphoreType.DMA((2,))]),
        out_shape=jax.ShapeDtypeStruct((b, n, d), F32),
        compiler_params=_cparams(3),
        name="moe_combine",
    )(lo, slot, y, h, post_ffn, g2)


def _rope_tables(n):
    t = jnp.arange(n, dtype=jnp.int32)
    pos_row = (t // GRID_W).astype(F32)
    pos_col = (t % GRID_W).astype(F32)
    n_freq = HEAD_DIM // 4
    inv_freq = ROPE_THETA ** (-jnp.arange(n_freq, dtype=F32) / n_freq)
    ang_r = pos_row[:, None] * inv_freq
    ang_c = pos_col[:, None] * inv_freq
    cos = jnp.concatenate([jnp.cos(ang_r)] * 2 + [jnp.cos(ang_c)] * 2, axis=1)
    sin = jnp.concatenate([-jnp.sin(ang_r), jnp.sin(ang_r), -jnp.sin(ang_c), jnp.sin(ang_c)], axis=1)
    return cos, sin


def kernel(x, c, ctx, c_ctx, w_mod, b_mod, pre_mix_norm, post_mix_norm, pre_ffn_norm, post_ffn_norm,
           w_in, q_norm, k_norm, sgu_norm, sgu_w, sgu_b, w_br_attn, w_br_sgu, w_out,
           w_router, w_gate, w_up, w_down):
    b, n, d = x.shape
    n_ctx = ctx.shape[1]
    assert w_mod.shape[0] == 1, "single-layer block"
    n_e = w_router.shape[-1]
    cap = EC_CAPACITY_FACTOR * n // n_e

    rows = -(-(b + 1) // SUBLANES) * SUBLANES
    cc = jnp.zeros((rows, d), F32).at[:b].set(c).at[b].set(c_ctx)
    mod = _mod_vectors(cc, w_mod[0], b_mod)
    sh1, sc1, g1, sh2, sc2, g2 = [mod[:, i * d:(i + 1) * d].reshape(rows, 1, d) for i in range(N_MOD)]

    a_lat = _norm_mod(x, pre_mix_norm, sh1, sc1, lambda bi: bi).reshape(b * n, d)
    a_ctx = _norm_mod(ctx, pre_mix_norm, sh1, sc1, lambda bi: b).reshape(b * n_ctx, d)

    w_in_b = w_in[0].astype(BF16)
    cos, sin = _rope_tables(n)
    lay = _proj_layout(d)
    proj = _in_proj(a_lat, w_in_b, cos, sin, k_norm, sgu_norm, n, d)
    k_c, v_c = _ctx_kv(a_ctx, w_in_b, k_norm, d)

    attn = _attention(proj.reshape(b, n, lay["end"]), k_c.reshape(b, n_ctx, KV_WIDTH),
                      v_c.reshape(b, n_ctx, KV_WIDTH), q_norm, cos, sin, lay)
    merged = _merge(attn.reshape(b * n, ATTN_WIDTH), proj, sgu_w[0], sgu_b[0].T,
                    w_br_attn[0].astype(BF16), w_br_sgu[0].astype(BF16), n, d, lay)
    h1, b_lat = _out_proj(merged, w_out[0].astype(BF16), x.reshape(b * n, d), post_mix_norm, g1,
                          pre_ffn_norm, sh2, sc2, n, d)
    h1 = h1.reshape(b, n, d)

    tile = min(n, MOE_TOKEN_TILE)
    aff = _router(b_lat.reshape(b, n, d), w_router[0].T)
    slot, gate, idx, lo = _select(aff, cap, tile)
    xin = _gather(idx.reshape(b * n_e, 1, cap), b_lat, n)
    hid = _expert_up(xin, w_gate[0], w_up[0])
    y = _expert_down(hid, w_down[0], gate)
    lo_flat = lo[:, :, :n // tile + 1].reshape(-1)
    return _combine(lo_flat, slot, y, h1, post_ffn_norm, g2, tile)
```

```python
import functools
import math

import jax
import jax.numpy as jnp
from jax import lax
from jax.experimental import pallas as pl
from jax.experimental.pallas import tpu as pltpu

F32 = jnp.float32
BF16 = jnp.bfloat16

GRID_W = 64
N_HEADS = 16
N_KV_HEADS = 4
HEAD_DIM = 128
GQA_GROUP = N_HEADS // N_KV_HEADS
ATTN_WIDTH = N_HEADS * HEAD_DIM
KV_WIDTH = N_KV_HEADS * HEAD_DIM
ROPE_THETA = 10000.0
SGU_CHUNK = 128
SGU_GROUPS = 8
EC_CAPACITY_FACTOR = 2
NORM_EPS = 1e-6
N_MOD = 6
QK_PRESCALE = HEAD_DIM ** -0.5 * math.log2(math.e)
ATTN_KEY_CHUNK = 512
ATTN_BLOCK_ROWS = 256
MOE_PIECE_ROWS = 16
MOE_CHUNK_ROWS = 256
MOE_TOKEN_TILE = 128
MOE_COMBINE_GROUP = 8

LANES = 128
SUBLANES = 8
VMEM_BUDGET_BYTES = 56 * 1024 * 1024


def _cparams(n_axes, vmem=VMEM_BUDGET_BYTES):
    return pltpu.CompilerParams(
        dimension_semantics=("arbitrary",) * n_axes, vmem_limit_bytes=vmem)


def _rms(x, gain):
    return x * lax.rsqrt(jnp.mean(x * x, axis=-1, keepdims=True) + NORM_EPS) * gain


def _dot(a, b):
    return jnp.dot(a, b, preferred_element_type=F32)


def _dot_nt(a, b):
    return lax.dot_general(a, b, (((1,), (1,)), ((), ())), preferred_element_type=F32)


def _dot_tn(a, b):
    return lax.dot_general(a, b, (((0,), (0,)), ((), ())), preferred_element_type=F32)


def _mod_kernel(c_ref, w_ref, b_ref, o_ref):
    c = c_ref[...]
    s = (c * jax.nn.sigmoid(c)).astype(BF16)
    o_ref[...] = _dot(s, w_ref[...].astype(BF16)) + b_ref[...]


def _mod_vectors(cc, w_mod, b_mod):
    rows, d = cc.shape
    n = w_mod.shape[1]
    tn = min(n, 512)
    return pl.pallas_call(
        _mod_kernel,
        grid=(n // tn,),
        in_specs=[pl.BlockSpec((rows, d), lambda j: (0, 0)),
                  pl.BlockSpec((d, tn), lambda j: (0, j)),
                  pl.BlockSpec((1, tn), lambda j: (0, j))],
        out_specs=pl.BlockSpec((rows, tn), lambda j: (0, j)),
        out_shape=jax.ShapeDtypeStruct((rows, n), F32),
        compiler_params=_cparams(1),
        name="mod_vectors",
    )(cc, w_mod, b_mod)


def _norm_mod_kernel(x_ref, g_ref, sh_ref, sc_ref, o_ref):
    x = x_ref[0]
    o_ref[0] = (_rms(x, g_ref[...]) * (1.0 + sc_ref[0]) + sh_ref[0]).astype(BF16)


def _norm_mod(x, gain, shift, scale, row_of_batch):
    b, n, d = x.shape
    tm = min(n, 512)
    return pl.pallas_call(
        _norm_mod_kernel,
        grid=(b, n // tm),
        in_specs=[pl.BlockSpec((1, tm, d), lambda bi, i: (bi, i, 0)),
                  pl.BlockSpec((1, d), lambda bi, i: (0, 0)),
                  pl.BlockSpec((1, 1, d), lambda bi, i: (row_of_batch(bi), 0, 0)),
                  pl.BlockSpec((1, 1, d), lambda bi, i: (row_of_batch(bi), 0, 0))],
        out_specs=pl.BlockSpec((1, tm, d), lambda bi, i: (bi, i, 0)),
        out_shape=jax.ShapeDtypeStruct((b, n, d), BF16),
        compiler_params=_cparams(2),
        name="norm_modulate",
    )(x, gain, shift, scale)


def _head_norm_rope(acc, gain, cos, sin, lane_lo):
    outs = []
    for h in range(acc.shape[1] // HEAD_DIM):
        y = _rms(acc[:, h * HEAD_DIM:(h + 1) * HEAD_DIM], gain)
        if cos is not None:
            partner = jnp.where(lane_lo, pltpu.roll(y, HEAD_DIM - HEAD_DIM // 4, axis=1),
                                pltpu.roll(y, HEAD_DIM // 4, axis=1))
            y = y * cos + partner * sin
        outs.append(y)
    return outs[0] if len(outs) == 1 else jnp.concatenate(outs, axis=1)


def _rope_lane_lo():
    lane = lax.broadcasted_iota(jnp.int32, (1, HEAD_DIM), 1)
    return (lane % (HEAD_DIM // 2)) < (HEAD_DIM // 4)


def _in_proj_kernel(kv_tile, s_tiles, group_dim, a_ref, w_ref, cos_ref, sin_ref, kn_ref, sn_ref, o_ref):
    j = pl.program_id(1)
    acc = _dot(a_ref[...], w_ref[...])
    s_lo, s_hi = s_tiles
    plain = (j != kv_tile) & ((j < s_lo) | (j >= s_hi))

    @pl.when(plain)
    def _():
        o_ref[...] = acc.astype(BF16)

    @pl.when(j == kv_tile)
    def _():
        k = _head_norm_rope(acc[:, :KV_WIDTH], kn_ref[...], cos_ref[...], sin_ref[...], _rope_lane_lo())
        o_ref[:, :KV_WIDTH] = k.astype(BF16)
        o_ref[:, KV_WIDTH:] = acc[:, KV_WIDTH:].astype(BF16)

    @pl.when((j >= s_lo) & (j < s_hi))
    def _():
        gain = sn_ref[...]
        for c in range(acc.shape[1] // group_dim):
            cols = slice(c * group_dim, (c + 1) * group_dim)
            o_ref[:, cols] = _rms(jax.nn.gelu(acc[:, cols]), gain[:, cols]).astype(BF16)


def _proj_layout(d):
    sw = d // 2
    names = ("q", "k", "v", "u", "s", "ga", "gb")
    widths = (ATTN_WIDTH, KV_WIDTH, KV_WIDTH, sw, sw, d, d)
    off, out = 0, {}
    for name, wd in zip(names, widths):
        out[name] = off
        off += wd
    out["end"] = off
    return out


def _in_proj(a, w, cos, sin, k_norm, sgu_norm, seq, d):
    m = a.shape[0]
    sw = d // 2
    tm = min(seq, 1024)
    tn = 2 * KV_WIDTH
    lay = _proj_layout(d)
    assert all(off % tn == 0 for name, off in lay.items() if name != "v") and lay["v"] == lay["k"] + KV_WIDTH
    s_lo = lay["s"] // tn
    s_tiles = (s_lo, lay["ga"] // tn)
    pos_tiles = seq // tm
    return pl.pallas_call(
        functools.partial(_in_proj_kernel, lay["k"] // tn, s_tiles, sw // SGU_GROUPS),
        grid=(m // tm, lay["end"] // tn),
        in_specs=[pl.BlockSpec((tm, d), lambda i, j: (i, 0)),
                  pl.BlockSpec((d, tn), lambda i, j: (0, j)),
                  pl.BlockSpec((tm, HEAD_DIM), lambda i, j: (i % pos_tiles, 0)),
                  pl.BlockSpec((tm, HEAD_DIM), lambda i, j: (i % pos_tiles, 0)),
                  pl.BlockSpec((1, HEAD_DIM), lambda i, j: (0, 0)),
                  pl.BlockSpec((1, tn), lambda i, j: (0, jnp.clip(j - s_lo, 0, s_tiles[1] - s_lo - 1)))],
        out_specs=pl.BlockSpec((tm, tn), lambda i, j: (i, j)),
        out_shape=jax.ShapeDtypeStruct((m, lay["end"]), BF16),
        compiler_params=_cparams(2),
        name="in_projection",
    )(a, w, cos, sin, k_norm, sgu_norm)


def _ctx_kv_kernel(a_ref, wk_ref, wv_ref, kn_ref, k_ref, v_ref):
    a = a_ref[...]
    k_ref[...] = _head_norm_rope(_dot(a, wk_ref[...]), kn_ref[...], None, None, None).astype(BF16)
    v_ref[...] = _dot(a, wv_ref[...]).astype(BF16)


def _ctx_kv(a, w, k_norm, d):
    m = a.shape[0]
    tm = min(m, 512)
    kb = ATTN_WIDTH // KV_WIDTH
    return pl.pallas_call(
        _ctx_kv_kernel,
        grid=(m // tm,),
        in_specs=[pl.BlockSpec((tm, d), lambda i: (i, 0)),
                  pl.BlockSpec((d, KV_WIDTH), lambda i: (0, kb)),
                  pl.BlockSpec((d, KV_WIDTH), lambda i: (0, kb + 1)),
                  pl.BlockSpec((1, HEAD_DIM), lambda i: (0, 0))],
        out_specs=[pl.BlockSpec((tm, KV_WIDTH), lambda i: (i, 0))] * 2,
        out_shape=[jax.ShapeDtypeStruct((m, KV_WIDTH), BF16)] * 2,
        compiler_params=_cparams(1),
        name="context_kv",
    )(a, w, w, k_norm)


def _lane_fold(x, op):
    out = x[:, :LANES]
    for c in range(1, x.shape[1] // LANES):
        out = op(out, x[:, c * LANES:(c + 1) * LANES])
    return out


def _attn_kernel(q_ref, kl_ref, vl_ref, kc_ref, vc_ref, qn_ref, cos_ref, sin_ref, o_ref, s_scr):
    tq = q_ref.shape[1]
    n, n_ctx = kl_ref.shape[1], kc_ref.shape[1]
    ck = min(n, ATTN_KEY_CHUNK)
    chunks = [(kl_ref, vl_ref, c * ck, ck, c * ck) for c in range(n // ck)] + [(kc_ref, vc_ref, 0, n_ctx, n)]
    lane_lo = _rope_lane_lo()
    rows_per_block = s_scr.shape[1]
    n_buf = s_scr.shape[0]
    blocks = [(g, r * rows_per_block) for r in range(tq // rows_per_block) for g in range(GQA_GROUP)]

    def scores(t):
        g, r0 = blocks[t]
        rows = slice(r0, r0 + rows_per_block)
        q = q_ref[0, rows, g * HEAD_DIM:(g + 1) * HEAD_DIM].astype(F32)
        q = _head_norm_rope(q, qn_ref[...], cos_ref[rows, :], sin_ref[rows, :], lane_lo)
        q = (q * QK_PRESCALE).astype(BF16)
        mvec = None
        for k_ref, _, k0, w, col in chunks:
            s = _dot_nt(q, k_ref[0, k0:k0 + w, :])
            s_scr[t % n_buf, :, col:col + w] = s
            part = _lane_fold(s, jnp.maximum)
            mvec = part if mvec is None else jnp.maximum(mvec, part)
        return jnp.max(mvec, axis=1, keepdims=True)

    def weighted_values(t, m):
        g, r0 = blocks[t]
        lvec = jnp.zeros((rows_per_block, LANES), F32)
        acc = jnp.zeros((rows_per_block, HEAD_DIM), F32)
        for _, v_ref, k0, w, col in chunks:
            p = jnp.exp2(s_scr[t % n_buf, :, col:col + w] - m)
            lvec = lvec + _lane_fold(p, jnp.add)
            acc = acc + _dot(p.astype(BF16), v_ref[0, k0:k0 + w, :])
        o = acc / jnp.sum(lvec, axis=1, keepdims=True)
        o_ref[0, r0:r0 + rows_per_block, g * HEAD_DIM:(g + 1) * HEAD_DIM] = o.astype(BF16)

    m_prev = None
    for t in range(len(blocks) + 1):
        m_cur = scores(t) if t < len(blocks) else None
        if t > 0:
            weighted_values(t - 1, m_prev)
        m_prev = m_cur


def _attention(proj, k_ctx, v_ctx, q_norm, cos, sin, lay):
    b, n, _ = proj.shape
    n_ctx = k_ctx.shape[1]
    tq = min(n, 2 * ATTN_BLOCK_ROWS)
    qw = GQA_GROUP * HEAD_DIM
    assert n % min(n, ATTN_KEY_CHUNK) == 0
    q0, k0, v0 = lay["q"] // qw, lay["k"] // HEAD_DIM, lay["v"] // HEAD_DIM
    return pl.pallas_call(
        _attn_kernel,
        grid=(b, N_KV_HEADS, n // tq),
        in_specs=[pl.BlockSpec((1, tq, qw), lambda bi, h, i: (bi, i, q0 + h)),
                  pl.BlockSpec((1, n, HEAD_DIM), lambda bi, h, i: (bi, 0, k0 + h)),
                  pl.BlockSpec((1, n, HEAD_DIM), lambda bi, h, i: (bi, 0, v0 + h)),
                  pl.BlockSpec((1, n_ctx, HEAD_DIM), lambda bi, h, i: (bi, 0, h)),
                  pl.BlockSpec((1, n_ctx, HEAD_DIM), lambda bi, h, i: (bi, 0, h)),
                  pl.BlockSpec((1, HEAD_DIM), lambda bi, h, i: (0, 0)),
                  pl.BlockSpec((tq, HEAD_DIM), lambda bi, h, i: (i, 0)),
                  pl.BlockSpec((tq, HEAD_DIM), lambda bi, h, i: (i, 0))],
        out_specs=pl.BlockSpec((1, tq, qw), lambda bi, h, i: (bi, i, h)),
        out_shape=jax.ShapeDtypeStruct((b, n, ATTN_WIDTH), BF16),
        scratch_shapes=[pltpu.VMEM((3, min(tq, ATTN_BLOCK_ROWS), n + n_ctx), F32)],
        compiler_params=_cparams(3),
        name="gqa_attention",
    )(proj, proj, proj, k_ctx, v_ctx, q_norm, cos, sin)


def _merge_kernel(group_dim, attn_ref, u0_ref, u1_ref, s0_ref, s1_ref, sw_ref, sbt_ref, wa_ref, ws_ref,
                  ga_ref, gb_ref, o_ref, sgu_scr):
    j = pl.program_id(1)
    tm = attn_ref.shape[0]
    half = SGU_GROUPS // 2

    @pl.when(j == 0)
    def _():
        for g in range(SGU_GROUPS):
            u_ref, s_ref = (u0_ref, s0_ref) if g < half else (u1_ref, s1_ref)
            wg = sw_ref[g].astype(BF16)
            bias = sbt_ref[:, g:g + 1]
            cols = slice((g % half) * group_dim, (g % half + 1) * group_dim)
            out_cols = slice(g * group_dim, (g + 1) * group_dim)
            for c in range(tm // SGU_CHUNK):
                rows = slice(c * SGU_CHUNK, (c + 1) * SGU_CHUNK)
                mixed = _dot(wg, s_ref[rows, cols]) + bias
                sgu_scr[rows, out_cols] = (jax.nn.gelu(u_ref[rows, cols].astype(F32)) * mixed).astype(BF16)

    ya = _dot(attn_ref[...], wa_ref[...])
    ys = _dot(sgu_scr[...], ws_ref[...])
    gate_a = jax.nn.sigmoid(ga_ref[...].astype(F32))
    gate_b = jax.nn.sigmoid(gb_ref[...].astype(F32))
    o_ref[...] = (gate_a * ya + gate_b * ys).astype(BF16)


def _merge(attn, proj, sgu_w, sgu_bt, w_br_attn, w_br_sgu, seq, d, lay):
    m = attn.shape[0]
    sw = d // 2
    hw = sw // 2
    tm = min(seq, 512)
    tn = 1024
    assert lay["u"] % hw == 0 and lay["s"] % hw == 0 and lay["ga"] % tn == 0 and lay["gb"] % tn == 0
    u0, s0, ga0, gb0 = lay["u"] // hw, lay["s"] // hw, lay["ga"] // tn, lay["gb"] // tn
    return pl.pallas_call(
        functools.partial(_merge_kernel, sw // SGU_GROUPS),
        grid=(m // tm, d // tn),
        in_specs=[pl.BlockSpec((tm, ATTN_WIDTH), lambda i, j: (i, 0)),
                  pl.BlockSpec((tm, hw), lambda i, j: (i, u0)),
                  pl.BlockSpec((tm, hw), lambda i, j: (i, u0 + 1)),
                  pl.BlockSpec((tm, hw), lambda i, j: (i, s0)),
                  pl.BlockSpec((tm, hw), lambda i, j: (i, s0 + 1)),
                  pl.BlockSpec((SGU_GROUPS, SGU_CHUNK, SGU_CHUNK), lambda i, j: (0, 0, 0)),
                  pl.BlockSpec((SGU_CHUNK, SGU_GROUPS), lambda i, j: (0, 0)),
                  pl.BlockSpec((ATTN_WIDTH, tn), lambda i, j: (0, j)),
                  pl.BlockSpec((sw, tn), lambda i, j: (0, j)),
                  pl.BlockSpec((tm, tn), lambda i, j: (i, ga0 + j)),
                  pl.BlockSpec((tm, tn), lambda i, j: (i, gb0 + j))],
        out_specs=pl.BlockSpec((tm, tn), lambda i, j: (i, j)),
        out_shape=jax.ShapeDtypeStruct((m, d), BF16),
        scratch_shapes=[pltpu.VMEM((tm, sw), BF16)],
        compiler_params=_cparams(2),
        name="sgu_branch_merge",
    )(attn, proj, proj, proj, proj, sgu_w, sgu_bt, w_br_attn, w_br_sgu, proj, proj)


def _out_proj_kernel(m_ref, w_ref, x_hbm, pmn_ref, g1_ref, pfn_ref, sh2_ref, sc2_ref,
                     h_hbm, b_hbm, y_scr, x_buf, h_buf, b_buf, sems):
    i, j = pl.program_id(0), pl.program_id(1)
    last_i, last_j = pl.num_programs(0) - 1, pl.num_programs(1) - 1
    tm = m_ref.shape[0]

    def rows(t):
        return pl.ds(pl.multiple_of(t * tm, tm), tm)

    def x_read():
        return pltpu.make_async_copy(x_hbm.at[rows(i), :], x_buf, sems.at[0])

    def h_write(t):
        return pltpu.make_async_copy(h_buf, h_hbm.at[rows(t), :], sems.at[1])

    def b_write(t):
        return pltpu.make_async_copy(b_buf, b_hbm.at[rows(t), :], sems.at[2])

    @pl.when(j == 0)
    def _():
        x_read().start()

    y_scr[j] = _dot(m_ref[...], w_ref[...])

    @pl.when(j == last_j)
    def _():
        x_read().wait()

        @pl.when(i > 0)
        def _():
            h_write(i - 1).wait()
            b_write(i - 1).wait()

        n_t, _, tn = y_scr.shape
        d = n_t * tn
        col = lambda t: slice(t * tn, (t + 1) * tn)
        ssq = sum(jnp.sum(y_scr[t] * y_scr[t], axis=1, keepdims=True) for t in range(n_t))
        y_inv = lax.rsqrt(ssq / d + NORM_EPS)
        ssq = jnp.zeros_like(ssq)
        for t in range(n_t):
            h = x_buf[:, col(t)] + g1_ref[0, :, col(t)] * (y_scr[t] * y_inv * pmn_ref[:, col(t)])
            h_buf[:, col(t)] = h
            ssq = ssq + jnp.sum(h * h, axis=1, keepdims=True)
        h_inv = lax.rsqrt(ssq / d + NORM_EPS)
        for t in range(n_t):
            b_buf[:, col(t)] = (h_buf[:, col(t)] * h_inv * pfn_ref[:, col(t)] * (1.0 + sc2_ref[0, :, col(t)])
                                + sh2_ref[0, :, col(t)])
        h_write(i).start()
        b_write(i).start()

        @pl.when(i == last_i)
        def _():
            h_write(i).wait()
            b_write(i).wait()


def _out_proj(merged, w_out, x, post_mix, g1, pre_ffn, sh2, sc2, seq, d):
    m = merged.shape[0]
    tm = min(seq, 512)
    tn = 512
    tiles_per_batch = seq // tm
    row = lambda i, j: (i // tiles_per_batch, 0, 0)
    any_spec = pl.BlockSpec(memory_space=pl.ANY)
    return pl.pallas_call(
        _out_proj_kernel,
        grid=(m // tm, d // tn),
        in_specs=[pl.BlockSpec((tm, d), lambda i, j: (i, 0)),
                  pl.BlockSpec((d, tn), lambda i, j: (0, j)),
                  any_spec,
                  pl.BlockSpec((1, d), lambda i, j: (0, 0)),
                  pl.BlockSpec((1, 1, d), row),
                  pl.BlockSpec((1, d), lambda i, j: (0, 0)),
                  pl.BlockSpec((1, 1, d), row),
                  pl.BlockSpec((1, 1, d), row)],
        out_specs=[any_spec, any_spec],
        out_shape=[jax.ShapeDtypeStruct((m, d), F32), jax.ShapeDtypeStruct((m, d), F32)],
        scratch_shapes=[pltpu.VMEM((d // tn, tm, tn), F32), pltpu.VMEM((tm, d), F32),
                        pltpu.VMEM((tm, d), F32), pltpu.VMEM((tm, d), F32),
                        pltpu.SemaphoreType.DMA((3,))],
        compiler_params=_cparams(2),
        name="out_projection",
    )(merged, w_out, x, post_mix, g1, pre_ffn, sh2, sc2)


def _split_bf16(x):
    hi = x.astype(BF16)
    return hi, (x - hi.astype(F32)).astype(BF16)


def _router_kernel(x_ref, wr_ref, aff_ref):
    n_e = wr_ref.shape[0]
    b_hi, b_lo = _split_bf16(x_ref[0])
    w_hi, w_lo = _split_bf16(wr_ref[...])
    p1 = _dot_nt(jnp.concatenate([w_hi, w_lo], axis=0), b_hi)
    logits = p1[:n_e] + p1[n_e:] + _dot_nt(w_hi, b_lo)
    z = jnp.exp(logits - jnp.max(logits, axis=0, keepdims=True))
    aff_ref[0] = z / jnp.sum(z, axis=0, keepdims=True)


def _router(b_lat, w_router_t):
    b, n, d = b_lat.shape
    n_e = w_router_t.shape[0]
    t = min(n, 512)
    return pl.pallas_call(
        _router_kernel,
        grid=(b, n // t),
        in_specs=[pl.BlockSpec((1, t, d), lambda bi, i: (bi, i, 0)),
                  pl.BlockSpec((n_e, d), lambda bi, i: (0, 0))],
        out_specs=pl.BlockSpec((1, n_e, t), lambda bi, i: (bi, 0, i)),
        out_shape=jax.ShapeDtypeStruct((b, n_e, n), F32),
        compiler_params=_cparams(2),
        name="router_affinity",
    )(b_lat, w_router_t)


def _prefix_count(x01, tri):
    rows, n = x01.shape
    off = jnp.zeros((rows, 1), F32)
    outs = []
    for c in range(n // LANES):
        y = _dot(x01[:, c * LANES:(c + 1) * LANES].astype(BF16), tri) + off
        outs.append(y)
        off = y[:, LANES - 1:LANES]
    return jnp.concatenate(outs, axis=1)


def _select_kernel(cap, tile_shift, aff_ref, slot_ref, gate_ref, idx_ref, lo_ref):
    aff = aff_ref[0]
    n_e, n = aff.shape
    bits = pltpu.bitcast(aff, jnp.int32)
    r = lax.broadcasted_iota(jnp.int32, (LANES, LANES), 0)
    c = lax.broadcasted_iota(jnp.int32, (LANES, LANES), 1)
    tri = (r <= c).astype(BF16)
    strict = (r < c).astype(BF16)

    def body(i, ans):
        cand = ans | lax.shift_left(jnp.int32(1), 30 - i)
        cnt = jnp.sum((bits >= cand).astype(F32), axis=1, keepdims=True)
        return jnp.where(cnt >= cap, cand, ans)

    kth = lax.fori_loop(0, 31, body, jnp.zeros((n_e, 1), jnp.int32))
    above = bits > kth
    tied = bits == kth
    need = cap - jnp.sum(above.astype(F32), axis=1, keepdims=True)
    tied_rank = _prefix_count(tied.astype(F32), tri)
    chosen = above | (tied & (tied_rank <= need))
    slot = _prefix_count(chosen.astype(F32), tri) - 1.0
    slot = jnp.where(chosen, slot, -1.0).astype(jnp.int32)
    slot_ref[0] = slot

    j = lax.broadcasted_iota(jnp.int32, (cap, 1), 0)
    tok = lax.broadcasted_iota(jnp.int32, (1, n), 1).astype(F32)
    for e in range(n_e):
        hit = slot[e:e + 1, :] == j
        gate_ref[0, e] = jnp.sum(jnp.where(hit, aff[e:e + 1, :], 0.0), axis=1, keepdims=True)
        idx_ref[0, e] = jnp.sum(jnp.where(hit, tok, 0.0), axis=1, keepdims=True).astype(jnp.int32)

    t_of = lax.shift_right_logical(lax.broadcasted_iota(jnp.int32, (n, LANES), 0), tile_shift)
    in_tile = (t_of == lax.broadcasted_iota(jnp.int32, (n, LANES), 1)).astype(BF16)
    per_tile = _dot(chosen.astype(BF16), in_tile)
    lo_ref[0] = _dot(per_tile.astype(BF16), strict).astype(jnp.int32)


def _select(aff, cap, tile):
    b, n_e, n = aff.shape
    tile_shift = tile.bit_length() - 1
    assert tile == 1 << tile_shift and tile <= 256 and n // tile < LANES
    return pl.pallas_call(
        functools.partial(_select_kernel, cap, tile_shift),
        grid=(b,),
        in_specs=[pl.BlockSpec((1, n_e, n), lambda bi: (bi, 0, 0))],
        out_specs=[pl.BlockSpec((1, n_e, n), lambda bi: (bi, 0, 0)),
                   pl.BlockSpec((1, n_e, cap, 1), lambda bi: (bi, 0, 0, 0)),
                   pl.BlockSpec((1, n_e, cap, 1), lambda bi: (bi, 0, 0, 0)),
                   pl.BlockSpec((1, n_e, LANES), lambda bi: (bi, 0, 0))],
        out_shape=[jax.ShapeDtypeStruct((b, n_e, n), jnp.int32),
                   jax.ShapeDtypeStruct((b, n_e, cap, 1), F32),
                   jax.ShapeDtypeStruct((b, n_e, cap, 1), jnp.int32),
                   jax.ShapeDtypeStruct((b, n_e, LANES), jnp.int32)],
        compiler_params=_cparams(1),
        name="expert_choice_select",
    )(aff)


def _gather_kernel(n_tok, idx_ref, idx_next_ref, src_ref, o_ref, buf, sems):
    cap = buf.shape[1]
    e, b = pl.program_id(0), pl.program_id(1)
    n_b = pl.num_programs(1)
    step = e * n_b + b
    cur = lax.rem(step, 2)

    def row_copy(idx, sample, j, half):
        return pltpu.make_async_copy(src_ref.at[pl.ds(sample * n_tok + idx[0, 0, j], 1), :],
                                     buf.at[half, pl.ds(j, 1), :], sems.at[half])

    def start_rows(idx, sample, half):
        def issue(j, carry):
            row_copy(idx, sample, j, half).start()
            return carry
        lax.fori_loop(0, cap, issue, 0, unroll=8)

    @pl.when(step == 0)
    def _():
        start_rows(idx_ref, b, cur)

    @pl.when(step < pl.num_programs(0) * n_b - 1)
    def _():
        start_rows(idx_next_ref, lax.rem(b + 1, n_b), 1 - cur)

    def drain(j, carry):
        row_copy(idx_ref, b, j, cur).wait()
        return carry

    lax.fori_loop(0, cap, drain, 0, unroll=8)
    o_ref[0, 0] = buf[cur].astype(BF16)


def _gather(idx, b_lat, n_tok):
    cap = idx.shape[-1]
    d = b_lat.shape[-1]
    b = b_lat.shape[0] // n_tok
    n_e = idx.shape[0] // b

    def idx_row(e, bi):
        return bi * n_e + e

    def next_idx_row(e, bi):
        wrap = (bi + 1) // b
        return idx_row(jnp.minimum(e + wrap, n_e - 1), (bi + 1) % b)

    return pl.pallas_call(
        functools.partial(_gather_kernel, n_tok),
        grid=(n_e, b),
        in_specs=[pl.BlockSpec((1, 1, cap), lambda e, bi: (idx_row(e, bi), 0, 0), memory_space=pltpu.SMEM),
                  pl.BlockSpec((1, 1, cap), lambda e, bi: (next_idx_row(e, bi), 0, 0),
                               memory_space=pltpu.SMEM),
                  pl.BlockSpec(memory_space=pl.ANY)],
        out_specs=pl.BlockSpec((1, 1, cap, d), lambda e, bi: (e, bi, 0, 0)),
        out_shape=jax.ShapeDtypeStruct((n_e, b, cap, d), BF16),
        scratch_shapes=[pltpu.VMEM((2, cap, d), F32), pltpu.SemaphoreType.DMA((2,))],
        compiler_params=_cparams(2),
        name="moe_gather",
    )(idx, idx, b_lat)


def _expert_up_kernel(x_ref, wg_ref, wu_ref, o_ref):
    wg = wg_ref[0].astype(BF16)
    wu = wu_ref[0].astype(BF16)
    for bi in range(x_ref.shape[1]):
        x = x_ref[0, bi]
        g = _dot(x, wg)
        o_ref[0, bi] = (g * jax.nn.sigmoid(g) * _dot(x, wu)).astype(BF16)


def _expert_up(xin, w_gate, w_up):
    n_e, b, cap, d = xin.shape
    f = w_gate.shape[-1]
    tn = min(f, 256)
    return pl.pallas_call(
        _expert_up_kernel,
        grid=(n_e, f // tn),
        in_specs=[pl.BlockSpec((1, b, cap, d), lambda e, j: (e, 0, 0, 0), pipeline_mode=pl.Buffered(1)),
                  pl.BlockSpec((1, d, tn), lambda e, j: (e, 0, j)),
                  pl.BlockSpec((1, d, tn), lambda e, j: (e, 0, j))],
        out_specs=pl.BlockSpec((1, b, cap, tn), lambda e, j: (e, 0, 0, j)),
        out_shape=jax.ShapeDtypeStruct((n_e, b, cap, f), BF16),
        compiler_params=_cparams(2),
        name="expert_up",
    )(xin, w_gate, w_up)


def _expert_down_kernel(h_ref, w_ref, g_ref, o_ref):
    w = w_ref[0].astype(BF16)
    for bi in range(h_ref.shape[1]):
        o_ref[0, bi] = (_dot(h_ref[0, bi], w) * g_ref[bi, 0]).astype(BF16)


def _expert_down(hid, w_down, gate):
    n_e, b, cap, f = hid.shape
    d = w_down.shape[-1]
    tn = min(d, 512)
    return pl.pallas_call(
        _expert_down_kernel,
        grid=(n_e, d // tn),
        in_specs=[pl.BlockSpec((1, b, cap, f), lambda e, j: (e, 0, 0, 0)),
                  pl.BlockSpec((1, f, tn), lambda e, j: (e, 0, j)),
                  pl.BlockSpec((b, 1, cap, 1), lambda e, j: (0, e, 0, 0))],
        out_specs=pl.BlockSpec((1, b, cap, tn), lambda e, j: (e, 0, 0, j)),
        out_shape=jax.ShapeDtypeStruct((n_e, b, cap, d), BF16),
        compiler_params=_cparams(2),
        name="expert_down",
    )(hid, w_down, gate)


def _combine_kernel(n_e, n_tiles, lo_ref, slot_ref, y_ref, h_ref, pfn_ref, g2_ref, o_ref, buf, acc, sems):
    bi, i, grp = pl.program_id(0), pl.program_id(1), pl.program_id(2)
    n_groups = pl.num_programs(2)
    group = slot_ref.shape[1]
    piece, chunk = MOE_PIECE_ROWS, MOE_CHUNK_ROWS
    shift = piece.bit_length() - 1
    step = (bi * n_tiles + i) * n_groups + grp
    cur = lax.rem(step, 2)

    def piece_ranges(sample, tile, g):
        out = []
        for k in range(group):
            e = g * group + k
            row = (sample * n_e + e) * (n_tiles + 1) + tile
            lo, hi = lo_ref[row], lo_ref[row + 1]
            first = lax.shift_right_logical(lo, shift)
            count = jnp.where(hi > lo, lax.shift_right_logical(hi + (piece - 1), shift) - first, 0)
            out.append((e, first, count))
        return out

    def piece_copy(sample, e, src_piece, half, dst_row):
        return pltpu.make_async_copy(
            y_ref.at[e, sample, pl.ds(pl.multiple_of(src_piece * piece, piece), piece), :],
            buf.at[half, pl.ds(pl.multiple_of(dst_row, piece), piece), :], sems.at[half])

    def start_fetch(sample, tile, g, half):
        filled = jnp.int32(0)
        for e, first, count in piece_ranges(sample, tile, g):
            def issue(p, carry, e=e, first=first, filled=filled):
                piece_copy(sample, e, first + p, half, filled + p * piece).start()
                return carry
            lax.fori_loop(0, count, issue, 0)
            filled = filled + count * piece

    @pl.when(step == 0)
    def _():
        buf[...] = jnp.zeros_like(buf)
        start_fetch(bi, i, grp, cur)

    @pl.when(step < pl.num_programs(0) * n_tiles * n_groups - 1)
    def _():
        wrap_g = grp == n_groups - 1
        nxt_i = jnp.where(wrap_g, i + 1, i)
        wrap_i = nxt_i == n_tiles
        start_fetch(jnp.where(wrap_i, bi + 1, bi), jnp.where(wrap_i, 0, nxt_i),
                    jnp.where(wrap_g, 0, grp + 1), 1 - cur)

    filled = jnp.int32(0)
    shifts = []
    for _, first, count in piece_ranges(bi, i, grp):
        shifts.append(filled - first * piece)
        filled = filled + count * piece

    def drain(p, carry):
        piece_copy(bi, 0, 0, cur, 0).wait()
        return carry

    lax.fori_loop(0, lax.shift_right_logical(filled, shift), drain, 0)

    slot = slot_ref[0]
    target = jnp.concatenate(
        [jnp.where(slot[k:k + 1, :] >= 0, slot[k:k + 1, :] + shifts[k], -1) for k in range(group)], axis=0)

    @pl.when(grp == 0)
    def _():
        acc[...] = jnp.zeros_like(acc)

    def add_chunk(c, carry):
        r0 = pl.multiple_of(c * chunk, chunk)
        rows = lax.broadcasted_iota(jnp.int32, (chunk, 1), 0) + r0
        hit = target[0:1, :] == rows
        for k in range(1, group):
            hit = hit | (target[k:k + 1, :] == rows)
        acc[...] += _dot_tn(hit.astype(BF16), buf[cur, pl.ds(r0, chunk), :])
        return carry

    lax.fori_loop(0, lax.shift_right_logical(filled + (chunk - 1), chunk.bit_length() - 1), add_chunk, 0)

    @pl.when(grp == n_groups - 1)
    def _():
        o_ref[0] = h_ref[0] + g2_ref[0] * _rms(acc[...], pfn_ref[...])


def _combine(lo, slot, y, h, post_ffn, g2, tile):
    b, n_e, n = slot.shape
    cap, d = y.shape[2], y.shape[3]
    n_tiles = n // tile
    group = MOE_COMBINE_GROUP
    assert cap % MOE_PIECE_ROWS == 0 and MOE_CHUNK_ROWS % MOE_PIECE_ROWS == 0 and n_e % group == 0
    buf_rows = group * (tile + 2 * MOE_PIECE_ROWS)
    buf_rows = -(-buf_rows // MOE_CHUNK_ROWS) * MOE_CHUNK_ROWS
    return pl.pallas_call(
        functools.partial(_combine_kernel, n_e, n_tiles),
        grid_spec=pltpu.PrefetchScalarGridSpec(
            num_scalar_prefetch=1,
            grid=(b, n_tiles, n_e // group),
            in_specs=[pl.BlockSpec((1, group, tile), lambda bi, i, g, lo_ref: (bi, g, i)),
                      pl.BlockSpec(memory_space=pl.ANY),
                      pl.BlockSpec((1, tile, d), lambda bi, i, g, lo_ref: (bi, i, 0)),
                      pl.BlockSpec((1, d), lambda bi, i, g, lo_ref: (0, 0)),
                      pl.BlockSpec((1, 1, d), lambda bi, i, g, lo_ref: (bi, 0, 0))],
            out_specs=pl.BlockSpec((1, tile, d), lambda bi, i, g, lo_ref: (bi, i, 0)),
            scratch_shapes=[pltpu.VMEM((2, buf_rows, d), BF16), pltpu.VMEM((tile, d), F32),
                            pltpu.SemaphoreType.DMA((2,))]),
        out_shape=jax.ShapeDtypeStruct((b, n, d), F32),
        compiler_params=_cparams(3),
        name="moe_combine",
    )(lo, slot, y, h, post_ffn, g2)


def _rope_tables(n):
    t = jnp.arange(n, dtype=jnp.int32)
    pos_row = (t // GRID_W).astype(F32)
    pos_col = (t % GRID_W).astype(F32)
    n_freq = HEAD_DIM // 4
    inv_freq = ROPE_THETA ** (-jnp.arange(n_freq, dtype=F32) / n_freq)
    ang_r = pos_row[:, None] * inv_freq
    ang_c = pos_col[:, None] * inv_freq
    cos = jnp.concatenate([jnp.cos(ang_r)] * 2 + [jnp.cos(ang_c)] * 2, axis=1)
    sin = jnp.concatenate([-jnp.sin(ang_r), jnp.sin(ang_r), -jnp.sin(ang_c), jnp.sin(ang_c)], axis=1)
    return cos, sin


def kernel(x, c, ctx, c_ctx, w_mod, b_mod, pre_mix_norm, post_mix_norm, pre_ffn_norm, post_ffn_norm,
           w_in, q_norm, k_norm, sgu_norm, sgu_w, sgu_b, w_br_attn, w_br_sgu, w_out,
           w_router, w_gate, w_up, w_down):
    b, n, d = x.shape
    n_ctx = ctx.shape[1]
    assert w_mod.shape[0] == 1, "single-layer block"
    n_e = w_router.shape[-1]
    cap = EC_CAPACITY_FACTOR * n // n_e

    rows = -(-(b + 1) // SUBLANES) * SUBLANES
    cc = jnp.zeros((rows, d), F32).at[:b].set(c).at[b].set(c_ctx)
    mod = _mod_vectors(cc, w_mod[0], b_mod)
    sh1, sc1, g1, sh2, sc2, g2 = [mod[:, i * d:(i + 1) * d].reshape(rows, 1, d) for i in range(N_MOD)]

    a_lat = _norm_mod(x, pre_mix_norm, sh1, sc1, lambda bi: bi).reshape(b * n, d)
    a_ctx = _norm_mod(ctx, pre_mix_norm, sh1, sc1, lambda bi: b).reshape(b * n_ctx, d)

    w_in_b = w_in[0].astype(BF16)
    cos, sin = _rope_tables(n)
    lay = _proj_layout(d)
    proj = _in_proj(a_lat, w_in_b, cos, sin, k_norm, sgu_norm, n, d)
    k_c, v_c = _ctx_kv(a_ctx, w_in_b, k_norm, d)

    attn = _attention(proj.reshape(b, n, lay["end"]), k_c.reshape(b, n_ctx, KV_WIDTH),
                      v_c.reshape(b, n_ctx, KV_WIDTH), q_norm, cos, sin, lay)
    merged = _merge(attn.reshape(b * n, ATTN_WIDTH), proj, sgu_w[0], sgu_b[0].T,
                    w_br_attn[0].astype(BF16), w_br_sgu[0].astype(BF16), n, d, lay)
    h1, b_lat = _out_proj(merged, w_out[0].astype(BF16), x.reshape(b * n, d), post_mix_norm, g1,
                          pre_ffn_norm, sh2, sc2, n, d)
    h1 = h1.reshape(b, n, d)

    tile = min(n, MOE_TOKEN_TILE)
    aff = _router(b_lat.reshape(b, n, d), w_router[0].T)
    slot, gate, idx, lo = _select(aff, cap, tile)
    xin = _gather(idx.reshape(b * n_e, 1, cap), b_lat, n)
    hid = _expert_up(xin, w_gate[0], w_up[0])
    y = _expert_down(hid, w_down[0], gate)
    lo_flat = lo[:, :, :n // tile + 1].reshape(-1)
    return _combine(lo_flat, slot, y, h1, post_ffn_norm, g2, tile)
```

```python
import functools
import math

import jax
import jax.numpy as jnp
from jax import lax
from jax.experimental import pallas as pl
from jax.experimental.pallas import tpu as pltpu

F32 = jnp.float32
BF16 = jnp.bfloat16

GRID_W = 64
N_HEADS = 16
N_KV_HEADS = 4
HEAD_DIM = 128
GQA_GROUP = N_HEADS // N_KV_HEADS
ATTN_WIDTH = N_HEADS * HEAD_DIM
KV_WIDTH = N_KV_HEADS * HEAD_DIM
ROPE_THETA = 10000.0
SGU_CHUNK = 128
SGU_GROUPS = 8
EC_CAPACITY_FACTOR = 2
NORM_EPS = 1e-6
N_MOD = 6
QK_PRESCALE = HEAD_DIM ** -0.5 * math.log2(math.e)
ATTN_KEY_CHUNK = 512
ATTN_BLOCK_ROWS = 256
OUT_PROJ_SLAB_ROWS = 64
MOE_PIECE_ROWS = 16
MOE_CHUNK_ROWS = 256
MOE_TOKEN_TILE = 128
MOE_COMBINE_GROUP = 8

LANES = 128
SUBLANES = 8
VMEM_BUDGET_BYTES = 56 * 1024 * 1024


def _cparams(n_axes, vmem=VMEM_BUDGET_BYTES):
    return pltpu.CompilerParams(
        dimension_semantics=("arbitrary",) * n_axes, vmem_limit_bytes=vmem)


def _rms(x, gain):
    return x * lax.rsqrt(jnp.mean(x * x, axis=-1, keepdims=True) + NORM_EPS) * gain


def _dot(a, b):
    return jnp.dot(a, b, preferred_element_type=F32)


def _dot_nt(a, b):
    return lax.dot_general(a, b, (((1,), (1,)), ((), ())), preferred_element_type=F32)


def _dot_tn(a, b):
    return lax.dot_general(a, b, (((0,), (0,)), ((), ())), preferred_element_type=F32)


def _mod_kernel(c_ref, w_ref, b_ref, o_ref):
    c = c_ref[...]
    s = (c * jax.nn.sigmoid(c)).astype(BF16)
    o_ref[...] = _dot(s, w_ref[...].astype(BF16)) + b_ref[...]


def _mod_vectors(cc, w_mod, b_mod):
    rows, d = cc.shape
    n = w_mod.shape[1]
    tn = min(n, 512)
    return pl.pallas_call(
        _mod_kernel,
        grid=(n // tn,),
        in_specs=[pl.BlockSpec((rows, d), lambda j: (0, 0)),
                  pl.BlockSpec((d, tn), lambda j: (0, j)),
                  pl.BlockSpec((1, tn), lambda j: (0, j))],
        out_specs=pl.BlockSpec((rows, tn), lambda j: (0, j)),
        out_shape=jax.ShapeDtypeStruct((rows, n), F32),
        compiler_params=_cparams(1),
        name="mod_vectors",
    )(cc, w_mod, b_mod)


def _norm_mod_kernel(x_ref, g_ref, sh_ref, sc_ref, o_ref):
    x = x_ref[0]
    o_ref[0] = (_rms(x, g_ref[...]) * (1.0 + sc_ref[0]) + sh_ref[0]).astype(BF16)


def _norm_mod(x, gain, shift, scale, row_of_batch):
    b, n, d = x.shape
    tm = min(n, 512)
    return pl.pallas_call(
        _norm_mod_kernel,
        grid=(b, n // tm),
        in_specs=[pl.BlockSpec((1, tm, d), lambda bi, i: (bi, i, 0)),
                  pl.BlockSpec((1, d), lambda bi, i: (0, 0)),
                  pl.BlockSpec((1, 1, d), lambda bi, i: (row_of_batch(bi), 0, 0)),
                  pl.BlockSpec((1, 1, d), lambda bi, i: (row_of_batch(bi), 0, 0))],
        out_specs=pl.BlockSpec((1, tm, d), lambda bi, i: (bi, i, 0)),
        out_shape=jax.ShapeDtypeStruct((b, n, d), BF16),
        compiler_params=_cparams(2),
        name="norm_modulate",
    )(x, gain, shift, scale)


def _head_norm_rope(acc, gain, cos, sin, lane_lo):
    outs = []
    for h in range(acc.shape[1] // HEAD_DIM):
        y = _rms(acc[:, h * HEAD_DIM:(h + 1) * HEAD_DIM], gain)
        if cos is not None:
            partner = jnp.where(lane_lo, pltpu.roll(y, HEAD_DIM - HEAD_DIM // 4, axis=1),
                                pltpu.roll(y, HEAD_DIM // 4, axis=1))
            y = y * cos + partner * sin
        outs.append(y)
    return outs[0] if len(outs) == 1 else jnp.concatenate(outs, axis=1)


def _rope_lane_lo():
    lane = lax.broadcasted_iota(jnp.int32, (1, HEAD_DIM), 1)
    return (lane % (HEAD_DIM // 2)) < (HEAD_DIM // 4)


def _in_proj_kernel(kv_tile, s_tiles, group_dim, a_ref, w_ref, cos_ref, sin_ref, kn_ref, sn_ref, o_ref):
    j = pl.program_id(1)
    acc = _dot(a_ref[...], w_ref[...])
    s_lo, s_hi = s_tiles
    plain = (j != kv_tile) & ((j < s_lo) | (j >= s_hi))

    @pl.when(plain)
    def _():
        o_ref[...] = acc.astype(BF16)

    @pl.when(j == kv_tile)
    def _():
        k = _head_norm_rope(acc[:, :KV_WIDTH], kn_ref[...], cos_ref[...], sin_ref[...], _rope_lane_lo())
        o_ref[:, :KV_WIDTH] = k.astype(BF16)
        o_ref[:, KV_WIDTH:] = acc[:, KV_WIDTH:].astype(BF16)

    @pl.when((j >= s_lo) & (j < s_hi))
    def _():
        gain = sn_ref[...]
        for c in range(acc.shape[1] // group_dim):
            cols = slice(c * group_dim, (c + 1) * group_dim)
            o_ref[:, cols] = _rms(jax.nn.gelu(acc[:, cols]), gain[:, cols]).astype(BF16)


def _proj_layout(d):
    sw = d // 2
    names = ("q", "k", "v", "u", "s", "ga", "gb")
    widths = (ATTN_WIDTH, KV_WIDTH, KV_WIDTH, sw, sw, d, d)
    off, out = 0, {}
    for name, wd in zip(names, widths):
        out[name] = off
        off += wd
    out["end"] = off
    return out


def _in_proj(a, w, cos, sin, k_norm, sgu_norm, seq, d):
    m = a.shape[0]
    sw = d // 2
    tm = min(seq, 1024)
    tn = 2 * KV_WIDTH
    lay = _proj_layout(d)
    assert all(off % tn == 0 for name, off in lay.items() if name != "v") and lay["v"] == lay["k"] + KV_WIDTH
    s_lo = lay["s"] // tn
    s_tiles = (s_lo, lay["ga"] // tn)
    pos_tiles = seq // tm
    return pl.pallas_call(
        functools.partial(_in_proj_kernel, lay["k"] // tn, s_tiles, sw // SGU_GROUPS),
        grid=(m // tm, lay["end"] // tn),
        in_specs=[pl.BlockSpec((tm, d), lambda i, j: (i, 0)),
                  pl.BlockSpec((d, tn), lambda i, j: (0, j)),
                  pl.BlockSpec((tm, HEAD_DIM), lambda i, j: (i % pos_tiles, 0)),
                  pl.BlockSpec((tm, HEAD_DIM), lambda i, j: (i % pos_tiles, 0)),
                  pl.BlockSpec((1, HEAD_DIM), lambda i, j: (0, 0)),
                  pl.BlockSpec((1, tn), lambda i, j: (0, jnp.clip(j - s_lo, 0, s_tiles[1] - s_lo - 1)))],
        out_specs=pl.BlockSpec((tm, tn), lambda i, j: (i, j)),
        out_shape=jax.ShapeDtypeStruct((m, lay["end"]), BF16),
        compiler_params=_cparams(2),
        name="in_projection",
    )(a, w, cos, sin, k_norm, sgu_norm)


def _ctx_kv_kernel(a_ref, wk_ref, wv_ref, kn_ref, k_ref, v_ref):
    a = a_ref[...]
    k_ref[...] = _head_norm_rope(_dot(a, wk_ref[...]), kn_ref[...], None, None, None).astype(BF16)
    v_ref[...] = _dot(a, wv_ref[...]).astype(BF16)


def _ctx_kv(a, w, k_norm, d):
    m = a.shape[0]
    tm = min(m, 512)
    kb = ATTN_WIDTH // KV_WIDTH
    return pl.pallas_call(
        _ctx_kv_kernel,
        grid=(m // tm,),
        in_specs=[pl.BlockSpec((tm, d), lambda i: (i, 0)),
                  pl.BlockSpec((d, KV_WIDTH), lambda i: (0, kb)),
                  pl.BlockSpec((d, KV_WIDTH), lambda i: (0, kb + 1)),
                  pl.BlockSpec((1, HEAD_DIM), lambda i: (0, 0))],
        out_specs=[pl.BlockSpec((tm, KV_WIDTH), lambda i: (i, 0))] * 2,
        out_shape=[jax.ShapeDtypeStruct((m, KV_WIDTH), BF16)] * 2,
        compiler_params=_cparams(1),
        name="context_kv",
    )(a, w, w, k_norm)


def _lane_fold(x, op):
    out = x[:, :LANES]
    for c in range(1, x.shape[1] // LANES):
        out = op(out, x[:, c * LANES:(c + 1) * LANES])
    return out


def _attn_kernel(q_ref, kl_ref, vl_ref, kc_ref, vc_ref, qn_ref, cos_ref, sin_ref, o_ref, s_scr):
    tq = q_ref.shape[1]
    n, n_ctx = kl_ref.shape[1], kc_ref.shape[1]
    ck = min(n, ATTN_KEY_CHUNK)
    chunks = [(kl_ref, vl_ref, c * ck, ck, c * ck) for c in range(n // ck)] + [(kc_ref, vc_ref, 0, n_ctx, n)]
    lane_lo = _rope_lane_lo()
    rows_per_block = s_scr.shape[1]
    n_buf = s_scr.shape[0]
    blocks = [(g, r * rows_per_block) for r in range(tq // rows_per_block) for g in range(GQA_GROUP)]

    def scores(t):
        g, r0 = blocks[t]
        rows = slice(r0, r0 + rows_per_block)
        q = q_ref[0, rows, g * HEAD_DIM:(g + 1) * HEAD_DIM].astype(F32)
        q = _head_norm_rope(q, qn_ref[...], cos_ref[rows, :], sin_ref[rows, :], lane_lo)
        q = (q * QK_PRESCALE).astype(BF16)
        mvec = None
        for k_ref, _, k0, w, col in chunks:
            s = _dot_nt(q, k_ref[0, k0:k0 + w, :])
            s_scr[t % n_buf, :, col:col + w] = s
            part = _lane_fold(s, jnp.maximum)
            mvec = part if mvec is None else jnp.maximum(mvec, part)
        return jnp.max(mvec, axis=1, keepdims=True)

    def weighted_values(t, m):
        g, r0 = blocks[t]
        lvec = jnp.zeros((rows_per_block, LANES), F32)
        acc = jnp.zeros((rows_per_block, HEAD_DIM), F32)
        for _, v_ref, k0, w, col in chunks:
            p = jnp.exp2(s_scr[t % n_buf, :, col:col + w] - m)
            lvec = lvec + _lane_fold(p, jnp.add)
            acc = acc + _dot(p.astype(BF16), v_ref[0, k0:k0 + w, :])
        o = acc / jnp.sum(lvec, axis=1, keepdims=True)
        o_ref[0, r0:r0 + rows_per_block, g * HEAD_DIM:(g + 1) * HEAD_DIM] = o.astype(BF16)

    m_prev = None
    for t in range(len(blocks) + 1):
        m_cur = scores(t) if t < len(blocks) else None
        if t > 0:
            weighted_values(t - 1, m_prev)
        m_prev = m_cur


def _attention(proj, k_ctx, v_ctx, q_norm, cos, sin, lay):
    b, n, _ = proj.shape
    n_ctx = k_ctx.shape[1]
    tq = min(n, 2 * ATTN_BLOCK_ROWS)
    qw = GQA_GROUP * HEAD_DIM
    assert n % min(n, ATTN_KEY_CHUNK) == 0
    q0, k0, v0 = lay["q"] // qw, lay["k"] // HEAD_DIM, lay["v"] // HEAD_DIM
    return pl.pallas_call(
        _attn_kernel,
        grid=(b, N_KV_HEADS, n // tq),
        in_specs=[pl.BlockSpec((1, tq, qw), lambda bi, h, i: (bi, i, q0 + h)),
                  pl.BlockSpec((1, n, HEAD_DIM), lambda bi, h, i: (bi, 0, k0 + h)),
                  pl.BlockSpec((1, n, HEAD_DIM), lambda bi, h, i: (bi, 0, v0 + h)),
                  pl.BlockSpec((1, n_ctx, HEAD_DIM), lambda bi, h, i: (bi, 0, h)),
                  pl.BlockSpec((1, n_ctx, HEAD_DIM), lambda bi, h, i: (bi, 0, h)),
                  pl.BlockSpec((1, HEAD_DIM), lambda bi, h, i: (0, 0)),
                  pl.BlockSpec((tq, HEAD_DIM), lambda bi, h, i: (i, 0)),
                  pl.BlockSpec((tq, HEAD_DIM), lambda bi, h, i: (i, 0))],
        out_specs=pl.BlockSpec((1, tq, qw), lambda bi, h, i: (bi, i, h)),
        out_shape=jax.ShapeDtypeStruct((b, n, ATTN_WIDTH), BF16),
        scratch_shapes=[pltpu.VMEM((3, min(tq, ATTN_BLOCK_ROWS), n + n_ctx), F32)],
        compiler_params=_cparams(3),
        name="gqa_attention",
    )(proj, proj, proj, k_ctx, v_ctx, q_norm, cos, sin)


def _merge_kernel(group_dim, attn_ref, u0_ref, u1_ref, s0_ref, s1_ref, sw_ref, sbt_ref, wa_ref, ws_ref,
                  ga_ref, gb_ref, o_ref, sgu_scr):
    j = pl.program_id(1)
    tm = attn_ref.shape[0]
    half = SGU_GROUPS // 2

    @pl.when(j == 0)
    def _():
        for g in range(SGU_GROUPS):
            u_ref, s_ref = (u0_ref, s0_ref) if g < half else (u1_ref, s1_ref)
            wg = sw_ref[g].astype(BF16)
            bias = sbt_ref[:, g:g + 1]
            cols = slice((g % half) * group_dim, (g % half + 1) * group_dim)
            out_cols = slice(g * group_dim, (g + 1) * group_dim)
            for c in range(tm // SGU_CHUNK):
                rows = slice(c * SGU_CHUNK, (c + 1) * SGU_CHUNK)
                mixed = _dot(wg, s_ref[rows, cols]) + bias
                sgu_scr[rows, out_cols] = (jax.nn.gelu(u_ref[rows, cols].astype(F32)) * mixed).astype(BF16)

    ya = _dot(attn_ref[...], wa_ref[...])
    ys = _dot(sgu_scr[...], ws_ref[...])
    gate_a = jax.nn.sigmoid(ga_ref[...].astype(F32))
    gate_b = jax.nn.sigmoid(gb_ref[...].astype(F32))
    o_ref[...] = (gate_a * ya + gate_b * ys).astype(BF16)


def _merge(attn, proj, sgu_w, sgu_bt, w_br_attn, w_br_sgu, seq, d, lay):
    m = attn.shape[0]
    sw = d // 2
    hw = sw // 2
    tm = min(seq, 512)
    tn = 1024
    assert lay["u"] % hw == 0 and lay["s"] % hw == 0 and lay["ga"] % tn == 0 and lay["gb"] % tn == 0
    u0, s0, ga0, gb0 = lay["u"] // hw, lay["s"] // hw, lay["ga"] // tn, lay["gb"] // tn
    return pl.pallas_call(
        functools.partial(_merge_kernel, sw // SGU_GROUPS),
        grid=(m // tm, d // tn),
        in_specs=[pl.BlockSpec((tm, ATTN_WIDTH), lambda i, j: (i, 0)),
                  pl.BlockSpec((tm, hw), lambda i, j: (i, u0)),
                  pl.BlockSpec((tm, hw), lambda i, j: (i, u0 + 1)),
                  pl.BlockSpec((tm, hw), lambda i, j: (i, s0)),
                  pl.BlockSpec((tm, hw), lambda i, j: (i, s0 + 1)),
                  pl.BlockSpec((SGU_GROUPS, SGU_CHUNK, SGU_CHUNK), lambda i, j: (0, 0, 0)),
                  pl.BlockSpec((SGU_CHUNK, SGU_GROUPS), lambda i, j: (0, 0)),
                  pl.BlockSpec((ATTN_WIDTH, tn), lambda i, j: (0, j)),
                  pl.BlockSpec((sw, tn), lambda i, j: (0, j)),
                  pl.BlockSpec((tm, tn), lambda i, j: (i, ga0 + j)),
                  pl.BlockSpec((tm, tn), lambda i, j: (i, gb0 + j))],
        out_specs=pl.BlockSpec((tm, tn), lambda i, j: (i, j)),
        out_shape=jax.ShapeDtypeStruct((m, d), BF16),
        scratch_shapes=[pltpu.VMEM((tm, sw), BF16)],
        compiler_params=_cparams(2),
        name="sgu_branch_merge",
    )(attn, proj, proj, proj, proj, sgu_w, sgu_bt, w_br_attn, w_br_sgu, proj, proj)


def _out_proj_kernel(m_ref, w_ref, x_hbm, pmn_ref, g1_ref, pfn_ref, sh2_ref, sc2_ref,
                     h_hbm, b_hbm, y_scr, ssq_scr, x_buf, h_buf, b_buf, sems):
    i, j = pl.program_id(0), pl.program_id(1)
    last_i, last_j = pl.num_programs(0) - 1, pl.num_programs(1) - 1
    tm = m_ref.shape[0]

    def rows(t):
        return pl.ds(pl.multiple_of(t * tm, tm), tm)

    def x_read():
        return pltpu.make_async_copy(x_hbm.at[rows(i), :], x_buf, sems.at[0])

    def h_write(t):
        return pltpu.make_async_copy(h_buf, h_hbm.at[rows(t), :], sems.at[1])

    def b_write(t):
        return pltpu.make_async_copy(b_buf, b_hbm.at[rows(t), :], sems.at[2])

    @pl.when(j == 0)
    def _():
        x_read().start()

    y = _dot(m_ref[...], w_ref[...])
    y_scr[j] = y
    y_sq = _lane_fold(y * y, jnp.add)

    @pl.when(j == 0)
    def _():
        ssq_scr[...] = y_sq

    @pl.when(j > 0)
    def _():
        ssq_scr[...] += y_sq

    @pl.when(j == last_j)
    def _():
        x_read().wait()

        @pl.when(i > 0)
        def _():
            h_write(i - 1).wait()
            b_write(i - 1).wait()

        n_t, _, tn = y_scr.shape
        d = n_t * tn
        col = lambda t: slice(t * tn, (t + 1) * tn)
        slab = OUT_PROJ_SLAB_ROWS

        def norm_rows(r, carry):
            rows = pl.ds(pl.multiple_of(r * slab, slab), slab)
            y_inv = lax.rsqrt(jnp.sum(ssq_scr[rows, :], axis=1, keepdims=True) / d + NORM_EPS)
            ssq = jnp.zeros((slab, 1), F32)
            for t in range(n_t):
                gain = g1_ref[0, :, col(t)] * pmn_ref[:, col(t)]
                h = x_buf[rows, col(t)] + (y_scr[t, rows, :] * y_inv) * gain
                h_buf[rows, col(t)] = h
                ssq = ssq + jnp.sum(h * h, axis=1, keepdims=True)
            h_inv = lax.rsqrt(ssq / d + NORM_EPS)
            for t in range(n_t):
                gain = pfn_ref[:, col(t)] * (1.0 + sc2_ref[0, :, col(t)])
                b_buf[rows, col(t)] = (h_buf[rows, col(t)] * h_inv) * gain + sh2_ref[0, :, col(t)]
            return carry

        lax.fori_loop(0, tm // slab, norm_rows, 0)
        h_write(i).start()
        b_write(i).start()

        @pl.when(i == last_i)
        def _():
            h_write(i).wait()
            b_write(i).wait()


def _out_proj(merged, w_out, x, post_mix, g1, pre_ffn, sh2, sc2, seq, d):
    m = merged.shape[0]
    tm = min(seq, 512)
    tn = 512
    tiles_per_batch = seq // tm
    row = lambda i, j: (i // tiles_per_batch, 0, 0)
    any_spec = pl.BlockSpec(memory_space=pl.ANY)
    return pl.pallas_call(
        _out_proj_kernel,
        grid=(m // tm, d // tn),
        in_specs=[pl.BlockSpec((tm, d), lambda i, j: (i, 0)),
                  pl.BlockSpec((d, tn), lambda i, j: (0, j)),
                  any_spec,
                  pl.BlockSpec((1, d), lambda i, j: (0, 0)),
                  pl.BlockSpec((1, 1, d), row),
                  pl.BlockSpec((1, d), lambda i, j: (0, 0)),
                  pl.BlockSpec((1, 1, d), row),
                  pl.BlockSpec((1, 1, d), row)],
        out_specs=[any_spec, any_spec],
        out_shape=[jax.ShapeDtypeStruct((m, d), F32), jax.ShapeDtypeStruct((m, d), F32)],
        scratch_shapes=[pltpu.VMEM((d // tn, tm, tn), F32), pltpu.VMEM((tm, LANES), F32),
                        pltpu.VMEM((tm, d), F32), pltpu.VMEM((tm, d), F32), pltpu.VMEM((tm, d), F32),
                        pltpu.SemaphoreType.DMA((3,))],
        compiler_params=_cparams(2),
        name="out_projection",
    )(merged, w_out, x, post_mix, g1, pre_ffn, sh2, sc2)


def _split_bf16(x):
    hi = x.astype(BF16)
    return hi, (x - hi.astype(F32)).astype(BF16)


def _router_kernel(x_ref, wr_ref, aff_ref):
    n_e = wr_ref.shape[0]
    b_hi, b_lo = _split_bf16(x_ref[0])
    w_hi, w_lo = _split_bf16(wr_ref[...])
    p1 = _dot_nt(jnp.concatenate([w_hi, w_lo], axis=0), b_hi)
    logits = p1[:n_e] + p1[n_e:] + _dot_nt(w_hi, b_lo)
    z = jnp.exp(logits - jnp.max(logits, axis=0, keepdims=True))
    aff_ref[0] = z / jnp.sum(z, axis=0, keepdims=True)


def _router(b_lat, w_router_t):
    b, n, d = b_lat.shape
    n_e = w_router_t.shape[0]
    t = min(n, 512)
    return pl.pallas_call(
        _router_kernel,
        grid=(b, n // t),
        in_specs=[pl.BlockSpec((1, t, d), lambda bi, i: (bi, i, 0)),
                  pl.BlockSpec((n_e, d), lambda bi, i: (0, 0))],
        out_specs=pl.BlockSpec((1, n_e, t), lambda bi, i: (bi, 0, i)),
        out_shape=jax.ShapeDtypeStruct((b, n_e, n), F32),
        compiler_params=_cparams(2),
        name="router_affinity",
    )(b_lat, w_router_t)


def _prefix_count(x01, tri):
    rows, n = x01.shape
    off = jnp.zeros((rows, 1), F32)
    outs = []
    for c in range(n // LANES):
        y = _dot(x01[:, c * LANES:(c + 1) * LANES].astype(BF16), tri) + off
        outs.append(y)
        off = y[:, LANES - 1:LANES]
    return jnp.concatenate(outs, axis=1)


def _select_kernel(cap, tile_shift, aff_ref, slot_ref, gate_ref, idx_ref, lo_ref):
    aff = aff_ref[0]
    n_e, n = aff.shape
    bits = pltpu.bitcast(aff, jnp.int32)
    r = lax.broadcasted_iota(jnp.int32, (LANES, LANES), 0)
    c = lax.broadcasted_iota(jnp.int32, (LANES, LANES), 1)
    tri = (r <= c).astype(BF16)
    strict = (r < c).astype(BF16)

    def body(i, ans):
        cand = ans | lax.shift_left(jnp.int32(1), 30 - i)
        cnt = jnp.sum((bits >= cand).astype(F32), axis=1, keepdims=True)
        return jnp.where(cnt >= cap, cand, ans)

    kth = lax.fori_loop(0, 31, body, jnp.zeros((n_e, 1), jnp.int32))
    above = bits > kth
    tied = bits == kth
    need = cap - jnp.sum(above.astype(F32), axis=1, keepdims=True)
    tied_rank = _prefix_count(tied.astype(F32), tri)
    chosen = above | (tied & (tied_rank <= need))
    slot = _prefix_count(chosen.astype(F32), tri) - 1.0
    slot = jnp.where(chosen, slot, -1.0).astype(jnp.int32)
    slot_ref[0] = slot

    j = lax.broadcasted_iota(jnp.int32, (cap, 1), 0)
    tok = lax.broadcasted_iota(jnp.int32, (1, n), 1).astype(F32)
    for e in range(n_e):
        hit = slot[e:e + 1, :] == j
        gate_ref[0, e] = jnp.sum(jnp.where(hit, aff[e:e + 1, :], 0.0), axis=1, keepdims=True)
        idx_ref[0, e] = jnp.sum(jnp.where(hit, tok, 0.0), axis=1, keepdims=True).astype(jnp.int32)

    t_of = lax.shift_right_logical(lax.broadcasted_iota(jnp.int32, (n, LANES), 0), tile_shift)
    in_tile = (t_of == lax.broadcasted_iota(jnp.int32, (n, LANES), 1)).astype(BF16)
    per_tile = _dot(chosen.astype(BF16), in_tile)
    lo_ref[0] = _dot(per_tile.astype(BF16), strict).astype(jnp.int32)


def _select(aff, cap, tile):
    b, n_e, n = aff.shape
    tile_shift = tile.bit_length() - 1
    assert tile == 1 << tile_shift and tile <= 256 and n // tile < LANES
    return pl.pallas_call(
        functools.partial(_select_kernel, cap, tile_shift),
        grid=(b,),
        in_specs=[pl.BlockSpec((1, n_e, n), lambda bi: (bi, 0, 0))],
        out_specs=[pl.BlockSpec((1, n_e, n), lambda bi: (bi, 0, 0)),
                   pl.BlockSpec((1, n_e, cap, 1), lambda bi: (bi, 0, 0, 0)),
                   pl.BlockSpec((1, n_e, cap, 1), lambda bi: (bi, 0, 0, 0)),
                   pl.BlockSpec((1, n_e, LANES), lambda bi: (bi, 0, 0))],
        out_shape=[jax.ShapeDtypeStruct((b, n_e, n), jnp.int32),
                   jax.ShapeDtypeStruct((b, n_e, cap, 1), F32),
                   jax.ShapeDtypeStruct((b, n_e, cap, 1), jnp.int32),
                   jax.ShapeDtypeStruct((b, n_e, LANES), jnp.int32)],
        compiler_params=_cparams(1),
        name="expert_choice_select",
    )(aff)


def _gather_kernel(n_tok, idx_ref, idx_next_ref, src_ref, o_ref, buf, sems):
    cap = buf.shape[1]
    e, b = pl.program_id(0), pl.program_id(1)
    n_b = pl.num_programs(1)
    step = e * n_b + b
    cur = lax.rem(step, 2)

    def row_copy(idx, sample, j, half):
        return pltpu.make_async_copy(src_ref.at[pl.ds(sample * n_tok + idx[0, 0, j], 1), :],
                                     buf.at[half, pl.ds(j, 1), :], sems.at[half])

    def start_rows(idx, sample, half):
        def issue(k, carry):
            j0 = pl.multiple_of(k * SUBLANES, SUBLANES)
            for u in range(SUBLANES):
                row_copy(idx, sample, j0 + u, half).start()
            return carry
        lax.fori_loop(0, cap // SUBLANES, issue, 0)

    @pl.when(step == 0)
    def _():
        start_rows(idx_ref, b, cur)

    @pl.when(step < pl.num_programs(0) * n_b - 1)
    def _():
        start_rows(idx_next_ref, lax.rem(b + 1, n_b), 1 - cur)

    def drain(j, carry):
        row_copy(idx_ref, b, j, cur).wait()
        return carry

    lax.fori_loop(0, cap, drain, 0, unroll=8)
    o_ref[0, 0] = buf[cur].astype(BF16)


def _gather(idx, b_lat, n_tok):
    cap = idx.shape[-1]
    d = b_lat.shape[-1]
    b = b_lat.shape[0] // n_tok
    n_e = idx.shape[0] // b

    def idx_row(e, bi):
        return bi * n_e + e

    def next_idx_row(e, bi):
        wrap = (bi + 1) // b
        return idx_row(jnp.minimum(e + wrap, n_e - 1), (bi + 1) % b)

    return pl.pallas_call(
        functools.partial(_gather_kernel, n_tok),
        grid=(n_e, b),
        in_specs=[pl.BlockSpec((1, 1, cap), lambda e, bi: (idx_row(e, bi), 0, 0), memory_space=pltpu.SMEM),
                  pl.BlockSpec((1, 1, cap), lambda e, bi: (next_idx_row(e, bi), 0, 0),
                               memory_space=pltpu.SMEM),
                  pl.BlockSpec(memory_space=pl.ANY)],
        out_specs=pl.BlockSpec((1, 1, cap, d), lambda e, bi: (e, bi, 0, 0)),
        out_shape=jax.ShapeDtypeStruct((n_e, b, cap, d), BF16),
        scratch_shapes=[pltpu.VMEM((2, cap, d), F32), pltpu.SemaphoreType.DMA((2,))],
        compiler_params=_cparams(2),
        name="moe_gather",
    )(idx, idx, b_lat)


def _expert_up_kernel(x_ref, wg_ref, wu_ref, o_ref):
    wg = wg_ref[0].astype(BF16)
    wu = wu_ref[0].astype(BF16)
    for bi in range(x_ref.shape[1]):
        x = x_ref[0, bi]
        g = _dot(x, wg)
        o_ref[0, bi] = (g * jax.nn.sigmoid(g) * _dot(x, wu)).astype(BF16)


def _expert_up(xin, w_gate, w_up):
    n_e, b, cap, d = xin.shape
    f = w_gate.shape[-1]
    tn = min(f, 256)
    return pl.pallas_call(
        _expert_up_kernel,
        grid=(n_e, f // tn),
        in_specs=[pl.BlockSpec((1, b, cap, d), lambda e, j: (e, 0, 0, 0), pipeline_mode=pl.Buffered(1)),
                  pl.BlockSpec((1, d, tn), lambda e, j: (e, 0, j)),
                  pl.BlockSpec((1, d, tn), lambda e, j: (e, 0, j))],
        out_specs=pl.BlockSpec((1, b, cap, tn), lambda e, j: (e, 0, 0, j)),
        out_shape=jax.ShapeDtypeStruct((n_e, b, cap, f), BF16),
        compiler_params=_cparams(2),
        name="expert_up",
    )(xin, w_gate, w_up)


def _expert_down_kernel(h_ref, w_ref, g_ref, o_ref):
    w = w_ref[0].astype(BF16)
    for bi in range(h_ref.shape[1]):
        o_ref[0, bi] = (_dot(h_ref[0, bi], w) * g_ref[bi, 0]).astype(BF16)


def _expert_down(hid, w_down, gate):
    n_e, b, cap, f = hid.shape
    d = w_down.shape[-1]
    tn = min(d, 512)
    return pl.pallas_call(
        _expert_down_kernel,
        grid=(n_e, d // tn),
        in_specs=[pl.BlockSpec((1, b, cap, f), lambda e, j: (e, 0, 0, 0)),
                  pl.BlockSpec((1, f, tn), lambda e, j: (e, 0, j)),
                  pl.BlockSpec((b, 1, cap, 1), lambda e, j: (0, e, 0, 0))],
        out_specs=pl.BlockSpec((1, b, cap, tn), lambda e, j: (e, 0, 0, j)),
        out_shape=jax.ShapeDtypeStruct((n_e, b, cap, d), BF16),
        compiler_params=_cparams(2),
        name="expert_down",
    )(hid, w_down, gate)


def _combine_kernel(n_e, n_tiles, lo_ref, slot_ref, y_ref, h_ref, pfn_ref, g2_ref, o_ref, buf, acc, sems):
    bi, i, grp = pl.program_id(0), pl.program_id(1), pl.program_id(2)
    n_groups = pl.num_programs(2)
    group = slot_ref.shape[1]
    piece, chunk = MOE_PIECE_ROWS, MOE_CHUNK_ROWS
    shift = piece.bit_length() - 1
    step = (bi * n_tiles + i) * n_groups + grp
    cur = lax.rem(step, 2)

    def piece_ranges(sample, tile, g):
        out = []
        for k in range(group):
            e = g * group + k
            row = (sample * n_e + e) * (n_tiles + 1) + tile
            lo, hi = lo_ref[row], lo_ref[row + 1]
            first = lax.shift_right_logical(lo, shift)
            count = jnp.where(hi > lo, lax.shift_right_logical(hi + (piece - 1), shift) - first, 0)
            out.append((e, first, count))
        return out

    def piece_copy(sample, e, src_piece, half, dst_row):
        return pltpu.make_async_copy(
            y_ref.at[e, sample, pl.ds(pl.multiple_of(src_piece * piece, piece), piece), :],
            buf.at[half, pl.ds(pl.multiple_of(dst_row, piece), piece), :], sems.at[half])

    def start_fetch(sample, tile, g, half):
        filled = jnp.int32(0)
        for e, first, count in piece_ranges(sample, tile, g):
            def issue(p, carry, e=e, first=first, filled=filled):
                piece_copy(sample, e, first + p, half, filled + p * piece).start()
                return carry
            lax.fori_loop(0, count, issue, 0)
            filled = filled + count * piece

    @pl.when(step == 0)
    def _():
        buf[...] = jnp.zeros_like(buf)
        start_fetch(bi, i, grp, cur)

    @pl.when(step < pl.num_programs(0) * n_tiles * n_groups - 1)
    def _():
        wrap_g = grp == n_groups - 1
        nxt_i = jnp.where(wrap_g, i + 1, i)
        wrap_i = nxt_i == n_tiles
        start_fetch(jnp.where(wrap_i, bi + 1, bi), jnp.where(wrap_i, 0, nxt_i),
                    jnp.where(wrap_g, 0, grp + 1), 1 - cur)

    filled = jnp.int32(0)
    shifts = []
    for _, first, count in piece_ranges(bi, i, grp):
        shifts.append(filled - first * piece)
        filled = filled + count * piece

    def drain(p, carry):
        piece_copy(bi, 0, 0, cur, 0).wait()
        return carry

    lax.fori_loop(0, lax.shift_right_logical(filled, shift), drain, 0)

    slot = slot_ref[0]
    target = jnp.concatenate(
        [jnp.where(slot[k:k + 1, :] >= 0, slot[k:k + 1, :] + shifts[k], -1) for k in range(group)], axis=0)

    @pl.when(grp == 0)
    def _():
        acc[...] = jnp.zeros_like(acc)

    def add_chunk(c, carry):
        r0 = pl.multiple_of(c * chunk, chunk)
        rows = lax.broadcasted_iota(jnp.int32, (chunk, 1), 0) + r0
        hit = target[0:1, :] == rows
        for k in range(1, group):
            hit = hit | (target[k:k + 1, :] == rows)
        acc[...] += _dot_tn(hit.astype(BF16), buf[cur, pl.ds(r0, chunk), :])
        return carry

    lax.fori_loop(0, lax.shift_right_logical(filled + (chunk - 1), chunk.bit_length() - 1), add_chunk, 0)

    @pl.when(grp == n_groups - 1)
    def _():
        o_ref[0] = h_ref[0] + g2_ref[0] * _rms(acc[...], pfn_ref[...])


def _combine(lo, slot, y, h, post_ffn, g2, tile):
    b, n_e, n = slot.shape
    cap, d = y.shape[2], y.shape[3]
    n_tiles = n // tile
    group = MOE_COMBINE_GROUP
    assert cap % MOE_PIECE_ROWS == 0 and MOE_CHUNK_ROWS % MOE_PIECE_ROWS == 0 and n_e % group == 0
    buf_rows = group * (tile + 2 * MOE_PIECE_ROWS)
    buf_rows = -(-buf_rows // MOE_CHUNK_ROWS) * MOE_CHUNK_ROWS
    return pl.pallas_call(
        functools.partial(_combine_kernel, n_e, n_tiles),
        grid_spec=pltpu.PrefetchScalarGridSpec(
            num_scalar_prefetch=1,
            grid=(b, n_tiles, n_e // group),
            in_specs=[pl.BlockSpec((1, group, tile), lambda bi, i, g, lo_ref: (bi, g, i)),
                      pl.BlockSpec(memory_space=pl.ANY),
                      pl.BlockSpec((1, tile, d), lambda bi, i, g, lo_ref: (bi, i, 0)),
                      pl.BlockSpec((1, d), lambda bi, i, g, lo_ref: (0, 0)),
                      pl.BlockSpec((1, 1, d), lambda bi, i, g, lo_ref: (bi, 0, 0))],
            out_specs=pl.BlockSpec((1, tile, d), lambda bi, i, g, lo_ref: (bi, i, 0)),
            scratch_shapes=[pltpu.VMEM((2, buf_rows, d), BF16), pltpu.VMEM((tile, d), F32),
                            pltpu.SemaphoreType.DMA((2,))]),
        out_shape=jax.ShapeDtypeStruct((b, n, d), F32),
        compiler_params=_cparams(3),
        name="moe_combine",
    )(lo, slot, y, h, post_ffn, g2)


def _rope_tables(n):
    t = jnp.arange(n, dtype=jnp.int32)
    pos_row = (t // GRID_W).astype(F32)
    pos_col = (t % GRID_W).astype(F32)
    n_freq = HEAD_DIM // 4
    inv_freq = ROPE_THETA ** (-jnp.arange(n_freq, dtype=F32) / n_freq)
    ang_r = pos_row[:, None] * inv_freq
    ang_c = pos_col[:, None] * inv_freq
    cos = jnp.concatenate([jnp.cos(ang_r)] * 2 + [jnp.cos(ang_c)] * 2, axis=1)
    sin = jnp.concatenate([-jnp.sin(ang_r), jnp.sin(ang_r), -jnp.sin(ang_c), jnp.sin(ang_c)], axis=1)
    return cos, sin


def kernel(x, c, ctx, c_ctx, w_mod, b_mod, pre_mix_norm, post_mix_norm, pre_ffn_norm, post_ffn_norm,
           w_in, q_norm, k_norm, sgu_norm, sgu_w, sgu_b, w_br_attn, w_br_sgu, w_out,
           w_router, w_gate, w_up, w_down):
    b, n, d = x.shape
    n_ctx = ctx.shape[1]
    assert w_mod.shape[0] == 1, "single-layer block"
    n_e = w_router.shape[-1]
    cap = EC_CAPACITY_FACTOR * n // n_e

    rows = -(-(b + 1) // SUBLANES) * SUBLANES
    cc = jnp.zeros((rows, d), F32).at[:b].set(c).at[b].set(c_ctx)
    mod = _mod_vectors(cc, w_mod[0], b_mod)
    sh1, sc1, g1, sh2, sc2, g2 = [mod[:, i * d:(i + 1) * d].reshape(rows, 1, d) for i in range(N_MOD)]

    a_lat = _norm_mod(x, pre_mix_norm, sh1, sc1, lambda bi: bi).reshape(b * n, d)
    a_ctx = _norm_mod(ctx, pre_mix_norm, sh1, sc1, lambda bi: b).reshape(b * n_ctx, d)

    w_in_b = w_in[0].astype(BF16)
    cos, sin = _rope_tables(n)
    lay = _proj_layout(d)
    proj = _in_proj(a_lat, w_in_b, cos, sin, k_norm, sgu_norm, n, d)
    k_c, v_c = _ctx_kv(a_ctx, w_in_b, k_norm, d)

    attn = _attention(proj.reshape(b, n, lay["end"]), k_c.reshape(b, n_ctx, KV_WIDTH),
                      v_c.reshape(b, n_ctx, KV_WIDTH), q_norm, cos, sin, lay)
    merged = _merge(attn.reshape(b * n, ATTN_WIDTH), proj, sgu_w[0], sgu_b[0].T,
                    w_br_attn[0].astype(BF16), w_br_sgu[0].astype(BF16), n, d, lay)
    h1, b_lat = _out_proj(merged, w_out[0].astype(BF16), x.reshape(b * n, d), post_mix_norm, g1,
                          pre_ffn_norm, sh2, sc2, n, d)
    h1 = h1.reshape(b, n, d)

    tile = min(n, MOE_TOKEN_TILE)
    aff = _router(b_lat.reshape(b, n, d), w_router[0].T)
    slot, gate, idx, lo = _select(aff, cap, tile)
    xin = _gather(idx.reshape(b * n_e, 1, cap), b_lat, n)
    hid = _expert_up(xin, w_gate[0], w_up[0])
    y = _expert_down(hid, w_down[0], gate)
    lo_flat = lo[:, :, :n // tile + 1].reshape(-1)
    return _combine(lo_flat, slot, y, h1, post_ffn_norm, g2, tile)
```

```python
import functools
import math

import jax
import jax.numpy as jnp
from jax import lax
from jax.experimental import pallas as pl
from jax.experimental.pallas import tpu as pltpu

F32 = jnp.float32
BF16 = jnp.bfloat16

GRID_W = 64
N_HEADS = 16
N_KV_HEADS = 4
HEAD_DIM = 128
GQA_GROUP = N_HEADS // N_KV_HEADS
ATTN_WIDTH = N_HEADS * HEAD_DIM
KV_WIDTH = N_KV_HEADS * HEAD_DIM
ROPE_THETA = 10000.0
SGU_CHUNK = 128
SGU_GROUPS = 8
EC_CAPACITY_FACTOR = 2
NORM_EPS = 1e-6
N_MOD = 6
QK_PRESCALE = HEAD_DIM ** -0.5 * math.log2(math.e)
ATTN_KEY_CHUNK = 512
ATTN_BLOCK_ROWS = 256
MOE_PIECE_ROWS = 16
MOE_CHUNK_ROWS = 256
MOE_TOKEN_TILE = 128
MOE_COMBINE_GROUP = 8

LANES = 128
SUBLANES = 8
VMEM_BUDGET_BYTES = 56 * 1024 * 1024


def _cparams(n_axes, vmem=VMEM_BUDGET_BYTES):
    return pltpu.CompilerParams(
        dimension_semantics=("arbitrary",) * n_axes, vmem_limit_bytes=vmem)


def _rms(x, gain):
    return x * lax.rsqrt(jnp.mean(x * x, axis=-1, keepdims=True) + NORM_EPS) * gain


def _dot(a, b):
    return jnp.dot(a, b, preferred_element_type=F32)


def _dot_nt(a, b):
    return lax.dot_general(a, b, (((1,), (1,)), ((), ())), preferred_element_type=F32)


def _dot_tn(a, b):
    return lax.dot_general(a, b, (((0,), (0,)), ((), ())), preferred_element_type=F32)


def _mod_kernel(c_ref, w_ref, b_ref, o_ref):
    c = c_ref[...]
    s = (c * jax.nn.sigmoid(c)).astype(BF16)
    o_ref[...] = _dot(s, w_ref[...].astype(BF16)) + b_ref[...]


def _mod_vectors(cc, w_mod, b_mod):
    rows, d = cc.shape
    n = w_mod.shape[1]
    tn = min(n, 512)
    return pl.pallas_call(
        _mod_kernel,
        grid=(n // tn,),
        in_specs=[pl.BlockSpec((rows, d), lambda j: (0, 0)),
                  pl.BlockSpec((d, tn), lambda j: (0, j)),
                  pl.BlockSpec((1, tn), lambda j: (0, j))],
        out_specs=pl.BlockSpec((rows, tn), lambda j: (0, j)),
        out_shape=jax.ShapeDtypeStruct((rows, n), F32),
        compiler_params=_cparams(1),
        name="mod_vectors",
    )(cc, w_mod, b_mod)


def _norm_mod_kernel(x_ref, g_ref, sh_ref, sc_ref, o_ref):
    x = x_ref[0]
    o_ref[0] = (_rms(x, g_ref[...]) * (1.0 + sc_ref[0]) + sh_ref[0]).astype(BF16)


def _norm_mod(x, gain, shift, scale, row_of_batch):
    b, n, d = x.shape
    tm = min(n, 512)
    return pl.pallas_call(
        _norm_mod_kernel,
        grid=(b, n // tm),
        in_specs=[pl.BlockSpec((1, tm, d), lambda bi, i: (bi, i, 0)),
                  pl.BlockSpec((1, d), lambda bi, i: (0, 0)),
                  pl.BlockSpec((1, 1, d), lambda bi, i: (row_of_batch(bi), 0, 0)),
                  pl.BlockSpec((1, 1, d), lambda bi, i: (row_of_batch(bi), 0, 0))],
        out_specs=pl.BlockSpec((1, tm, d), lambda bi, i: (bi, i, 0)),
        out_shape=jax.ShapeDtypeStruct((b, n, d), BF16),
        compiler_params=_cparams(2),
        name="norm_modulate",
    )(x, gain, shift, scale)


def _head_norm_rope(acc, gain, cos, sin, lane_lo):
    outs = []
    for h in range(acc.shape[1] // HEAD_DIM):
        y = _rms(acc[:, h * HEAD_DIM:(h + 1) * HEAD_DIM], gain)
        if cos is not None:
            partner = jnp.where(lane_lo, pltpu.roll(y, HEAD_DIM - HEAD_DIM // 4, axis=1),
                                pltpu.roll(y, HEAD_DIM // 4, axis=1))
            y = y * cos + partner * sin
        outs.append(y)
    return outs[0] if len(outs) == 1 else jnp.concatenate(outs, axis=1)


def _rope_lane_lo():
    lane = lax.broadcasted_iota(jnp.int32, (1, HEAD_DIM), 1)
    return (lane % (HEAD_DIM // 2)) < (HEAD_DIM // 4)


def _in_proj_kernel(kv_tile, s_tiles, group_dim, a_ref, w_ref, cos_ref, sin_ref, kn_ref, sn_ref, o_ref):
    j = pl.program_id(1)
    acc = _dot(a_ref[...], w_ref[...])
    s_lo, s_hi = s_tiles
    plain = (j != kv_tile) & ((j < s_lo) | (j >= s_hi))

    @pl.when(plain)
    def _():
        o_ref[...] = acc.astype(BF16)

    @pl.when(j == kv_tile)
    def _():
        k = _head_norm_rope(acc[:, :KV_WIDTH], kn_ref[...], cos_ref[...], sin_ref[...], _rope_lane_lo())
        o_ref[:, :KV_WIDTH] = k.astype(BF16)
        o_ref[:, KV_WIDTH:] = acc[:, KV_WIDTH:].astype(BF16)

    @pl.when((j >= s_lo) & (j < s_hi))
    def _():
        gain = sn_ref[...]
        for c in range(acc.shape[1] // group_dim):
            cols = slice(c * group_dim, (c + 1) * group_dim)
            o_ref[:, cols] = _rms(jax.nn.gelu(acc[:, cols]), gain[:, cols]).astype(BF16)


def _proj_layout(d):
    sw = d // 2
    names = ("q", "k", "v", "u", "s", "ga", "gb")
    widths = (ATTN_WIDTH, KV_WIDTH, KV_WIDTH, sw, sw, d, d)
    off, out = 0, {}
    for name, wd in zip(names, widths):
        out[name] = off
        off += wd
    out["end"] = off
    return out


def _in_proj(a, w, cos, sin, k_norm, sgu_norm, seq, d):
    m = a.shape[0]
    sw = d // 2
    tm = min(seq, 1024)
    tn = 2 * KV_WIDTH
    lay = _proj_layout(d)
    assert all(off % tn == 0 for name, off in lay.items() if name != "v") and lay["v"] == lay["k"] + KV_WIDTH
    s_lo = lay["s"] // tn
    s_tiles = (s_lo, lay["ga"] // tn)
    pos_tiles = seq // tm
    return pl.pallas_call(
        functools.partial(_in_proj_kernel, lay["k"] // tn, s_tiles, sw // SGU_GROUPS),
        grid=(m // tm, lay["end"] // tn),
        in_specs=[pl.BlockSpec((tm, d), lambda i, j: (i, 0)),
                  pl.BlockSpec((d, tn), lambda i, j: (0, j)),
                  pl.BlockSpec((tm, HEAD_DIM), lambda i, j: (i % pos_tiles, 0)),
                  pl.BlockSpec((tm, HEAD_DIM), lambda i, j: (i % pos_tiles, 0)),
                  pl.BlockSpec((1, HEAD_DIM), lambda i, j: (0, 0)),
                  pl.BlockSpec((1, tn), lambda i, j: (0, jnp.clip(j - s_lo, 0, s_tiles[1] - s_lo - 1)))],
        out_specs=pl.BlockSpec((tm, tn), lambda i, j: (i, j)),
        out_shape=jax.ShapeDtypeStruct((m, lay["end"]), BF16),
        compiler_params=_cparams(2),
        name="in_projection",
    )(a, w, cos, sin, k_norm, sgu_norm)


def _ctx_kv_kernel(a_ref, wk_ref, wv_ref, kn_ref, k_ref, v_ref):
    a = a_ref[...]
    k_ref[...] = _head_norm_rope(_dot(a, wk_ref[...]), kn_ref[...], None, None, None).astype(BF16)
    v_ref[...] = _dot(a, wv_ref[...]).astype(BF16)


def _ctx_kv(a, w, k_norm, d):
    m = a.shape[0]
    tm = min(m, 512)
    kb = ATTN_WIDTH // KV_WIDTH
    return pl.pallas_call(
        _ctx_kv_kernel,
        grid=(m // tm,),
        in_specs=[pl.BlockSpec((tm, d), lambda i: (i, 0)),
                  pl.BlockSpec((d, KV_WIDTH), lambda i: (0, kb)),
                  pl.BlockSpec((d, KV_WIDTH), lambda i: (0, kb + 1)),
                  pl.BlockSpec((1, HEAD_DIM), lambda i: (0, 0))],
        out_specs=[pl.BlockSpec((tm, KV_WIDTH), lambda i: (i, 0))] * 2,
        out_shape=[jax.ShapeDtypeStruct((m, KV_WIDTH), BF16)] * 2,
        compiler_params=_cparams(1),
        name="context_kv",
    )(a, w, w, k_norm)


def _lane_fold(x, op):
    out = x[:, :LANES]
    for c in range(1, x.shape[1] // LANES):
        out = op(out, x[:, c * LANES:(c + 1) * LANES])
    return out


def _attn_kernel(q_ref, kl_ref, vl_ref, kc_ref, vc_ref, qn_ref, cos_ref, sin_ref, o_ref, s_scr):
    tq = q_ref.shape[1]
    n, n_ctx = kl_ref.shape[1], kc_ref.shape[1]
    ck = min(n, ATTN_KEY_CHUNK)
    chunks = [(kl_ref, vl_ref, c * ck, ck, c * ck) for c in range(n // ck)] + [(kc_ref, vc_ref, 0, n_ctx, n)]
    lane_lo = _rope_lane_lo()
    rows_per_block = s_scr.shape[1]
    n_buf = s_scr.shape[0]
    blocks = [(g, r * rows_per_block) for r in range(tq // rows_per_block) for g in range(GQA_GROUP)]

    def scores(t):
        g, r0 = blocks[t]
        rows = slice(r0, r0 + rows_per_block)
        q = q_ref[0, rows, g * HEAD_DIM:(g + 1) * HEAD_DIM].astype(F32)
        q = _head_norm_rope(q, qn_ref[...], cos_ref[rows, :], sin_ref[rows, :], lane_lo)
        q = (q * QK_PRESCALE).astype(BF16)
        mvec = None
        for k_ref, _, k0, w, col in chunks:
            s = _dot_nt(q, k_ref[0, k0:k0 + w, :])
            s_scr[t % n_buf, :, col:col + w] = s
            part = _lane_fold(s, jnp.maximum)
            mvec = part if mvec is None else jnp.maximum(mvec, part)
        return jnp.max(mvec, axis=1, keepdims=True)

    def weighted_values(t, m):
        g, r0 = blocks[t]
        lvec = jnp.zeros((rows_per_block, LANES), F32)
        acc = jnp.zeros((rows_per_block, HEAD_DIM), F32)
        for _, v_ref, k0, w, col in chunks:
            p = jnp.exp2(s_scr[t % n_buf, :, col:col + w] - m)
            lvec = lvec + _lane_fold(p, jnp.add)
            acc = acc + _dot(p.astype(BF16), v_ref[0, k0:k0 + w, :])
        o = acc / jnp.sum(lvec, axis=1, keepdims=True)
        o_ref[0, r0:r0 + rows_per_block, g * HEAD_DIM:(g + 1) * HEAD_DIM] = o.astype(BF16)

    m_prev = None
    for t in range(len(blocks) + 1):
        m_cur = scores(t) if t < len(blocks) else None
        if t > 0:
            weighted_values(t - 1, m_prev)
        m_prev = m_cur


def _attention(proj, k_ctx, v_ctx, q_norm, cos, sin, lay):
    b, n, _ = proj.shape
    n_ctx = k_ctx.shape[1]
    tq = min(n, 2 * ATTN_BLOCK_ROWS)
    qw = GQA_GROUP * HEAD_DIM
    assert n % min(n, ATTN_KEY_CHUNK) == 0
    q0, k0, v0 = lay["q"] // qw, lay["k"] // HEAD_DIM, lay["v"] // HEAD_DIM
    return pl.pallas_call(
        _attn_kernel,
        grid=(b, N_KV_HEADS, n // tq),
        in_specs=[pl.BlockSpec((1, tq, qw), lambda bi, h, i: (bi, i, q0 + h)),
                  pl.BlockSpec((1, n, HEAD_DIM), lambda bi, h, i: (bi, 0, k0 + h)),
                  pl.BlockSpec((1, n, HEAD_DIM), lambda bi, h, i: (bi, 0, v0 + h)),
                  pl.BlockSpec((1, n_ctx, HEAD_DIM), lambda bi, h, i: (bi, 0, h)),
                  pl.BlockSpec((1, n_ctx, HEAD_DIM), lambda bi, h, i: (bi, 0, h)),
                  pl.BlockSpec((1, HEAD_DIM), lambda bi, h, i: (0, 0)),
                  pl.BlockSpec((tq, HEAD_DIM), lambda bi, h, i: (i, 0)),
                  pl.BlockSpec((tq, HEAD_DIM), lambda bi, h, i: (i, 0))],
        out_specs=pl.BlockSpec((1, tq, qw), lambda bi, h, i: (bi, i, h)),
        out_shape=jax.ShapeDtypeStruct((b, n, ATTN_WIDTH), BF16),
        scratch_shapes=[pltpu.VMEM((3, min(tq, ATTN_BLOCK_ROWS), n + n_ctx), F32)],
        compiler_params=_cparams(3),
        name="gqa_attention",
    )(proj, proj, proj, k_ctx, v_ctx, q_norm, cos, sin)


def _merge_kernel(group_dim, attn_ref, u0_ref, u1_ref, s0_ref, s1_ref, sw_ref, sbt_ref, wa_ref, ws_ref,
                  ga_ref, gb_ref, o_ref, sgu_scr):
    j = pl.program_id(1)
    tm = attn_ref.shape[0]
    half = SGU_GROUPS // 2

    @pl.when(j == 0)
    def _():
        for g in range(SGU_GROUPS):
            u_ref, s_ref = (u0_ref, s0_ref) if g < half else (u1_ref, s1_ref)
            wg = sw_ref[g].astype(BF16)
            bias = sbt_ref[:, g:g + 1]
            cols = slice((g % half) * group_dim, (g % half + 1) * group_dim)
            out_cols = slice(g * group_dim, (g + 1) * group_dim)
            for c in range(tm // SGU_CHUNK):
                rows = slice(c * SGU_CHUNK, (c + 1) * SGU_CHUNK)
                mixed = _dot(wg, s_ref[rows, cols]) + bias
                sgu_scr[rows, out_cols] = (jax.nn.gelu(u_ref[rows, cols].astype(F32)) * mixed).astype(BF16)

    ya = _dot(attn_ref[...], wa_ref[...])
    ys = _dot(sgu_scr[...], ws_ref[...])
    gate_a = jax.nn.sigmoid(ga_ref[...].astype(F32))
    gate_b = jax.nn.sigmoid(gb_ref[...].astype(F32))
    o_ref[...] = (gate_a * ya + gate_b * ys).astype(BF16)


def _merge(attn, proj, sgu_w, sgu_bt, w_br_attn, w_br_sgu, seq, d, lay):
    m = attn.shape[0]
    sw = d // 2
    hw = sw // 2
    tm = min(seq, 512)
    tn = 1024
    assert lay["u"] % hw == 0 and lay["s"] % hw == 0 and lay["ga"] % tn == 0 and lay["gb"] % tn == 0
    u0, s0, ga0, gb0 = lay["u"] // hw, lay["s"] // hw, lay["ga"] // tn, lay["gb"] // tn
    return pl.pallas_call(
        functools.partial(_merge_kernel, sw // SGU_GROUPS),
        grid=(m // tm, d // tn),
        in_specs=[pl.BlockSpec((tm, ATTN_WIDTH), lambda i, j: (i, 0)),
                  pl.BlockSpec((tm, hw), lambda i, j: (i, u0)),
                  pl.BlockSpec((tm, hw), lambda i, j: (i, u0 + 1)),
                  pl.BlockSpec((tm, hw), lambda i, j: (i, s0)),
                  pl.BlockSpec((tm, hw), lambda i, j: (i, s0 + 1)),
                  pl.BlockSpec((SGU_GROUPS, SGU_CHUNK, SGU_CHUNK), lambda i, j: (0, 0, 0)),
                  pl.BlockSpec((SGU_CHUNK, SGU_GROUPS), lambda i, j: (0, 0)),
                  pl.BlockSpec((ATTN_WIDTH, tn), lambda i, j: (0, j)),
                  pl.BlockSpec((sw, tn), lambda i, j: (0, j)),
                  pl.BlockSpec((tm, tn), lambda i, j: (i, ga0 + j)),
                  pl.BlockSpec((tm, tn), lambda i, j: (i, gb0 + j))],
        out_specs=pl.BlockSpec((tm, tn), lambda i, j: (i, j)),
        out_shape=jax.ShapeDtypeStruct((m, d), BF16),
        scratch_shapes=[pltpu.VMEM((tm, sw), BF16)],
        compiler_params=_cparams(2),
        name="sgu_branch_merge",
    )(attn, proj, proj, proj, proj, sgu_w, sgu_bt, w_br_attn, w_br_sgu, proj, proj)


def _out_proj_kernel(m_ref, w_ref, x_hbm, pmn_ref, g1_ref, pfn_ref, sh2_ref, sc2_ref,
                     h_hbm, b_hbm, y_scr, x_buf, h_buf, b_buf, sems):
    i, j = pl.program_id(0), pl.program_id(1)
    last_i, last_j = pl.num_programs(0) - 1, pl.num_programs(1) - 1
    tm = m_ref.shape[0]

    def rows(t):
        return pl.ds(pl.multiple_of(t * tm, tm), tm)

    def x_read():
        return pltpu.make_async_copy(x_hbm.at[rows(i), :], x_buf, sems.at[0])

    def h_write(t):
        return pltpu.make_async_copy(h_buf, h_hbm.at[rows(t), :], sems.at[1])

    def b_write(t):
        return pltpu.make_async_copy(b_buf, b_hbm.at[rows(t), :], sems.at[2])

    @pl.when(j == 0)
    def _():
        x_read().start()

    y_scr[j] = _dot(m_ref[...], w_ref[...])

    @pl.when(j == last_j)
    def _():
        x_read().wait()

        @pl.when(i > 0)
        def _():
            h_write(i - 1).wait()
            b_write(i - 1).wait()

        n_t, _, tn = y_scr.shape
        d = n_t * tn
        col = lambda t: slice(t * tn, (t + 1) * tn)
        ssq = sum(jnp.sum(y_scr[t] * y_scr[t], axis=1, keepdims=True) for t in range(n_t))
        y_inv = lax.rsqrt(ssq / d + NORM_EPS)
        ssq = jnp.zeros_like(ssq)
        for t in range(n_t):
            h = x_buf[:, col(t)] + g1_ref[0, :, col(t)] * (y_scr[t] * y_inv * pmn_ref[:, col(t)])
            h_buf[:, col(t)] = h
            ssq = ssq + jnp.sum(h * h, axis=1, keepdims=True)
        h_inv = lax.rsqrt(ssq / d + NORM_EPS)
        for t in range(n_t):
            b_buf[:, col(t)] = (h_buf[:, col(t)] * h_inv * pfn_ref[:, col(t)] * (1.0 + sc2_ref[0, :, col(t)])
                                + sh2_ref[0, :, col(t)])
        h_write(i).start()
        b_write(i).start()

        @pl.when(i == last_i)
        def _():
            h_write(i).wait()
            b_write(i).wait()


def _out_proj(merged, w_out, x, post_mix, g1, pre_ffn, sh2, sc2, seq, d):
    m = merged.shape[0]
    tm = min(seq, 512)
    tn = 512
    tiles_per_batch = seq // tm
    row = lambda i, j: (i // tiles_per_batch, 0, 0)
    any_spec = pl.BlockSpec(memory_space=pl.ANY)
    return pl.pallas_call(
        _out_proj_kernel,
        grid=(m // tm, d // tn),
        in_specs=[pl.BlockSpec((tm, d), lambda i, j: (i, 0)),
                  pl.BlockSpec((d, tn), lambda i, j: (0, j)),
                  any_spec,
                  pl.BlockSpec((1, d), lambda i, j: (0, 0)),
                  pl.BlockSpec((1, 1, d), row),
                  pl.BlockSpec((1, d), lambda i, j: (0, 0)),
                  pl.BlockSpec((1, 1, d), row),
                  pl.BlockSpec((1, 1, d), row)],
        out_specs=[any_spec, any_spec],
        out_shape=[jax.ShapeDtypeStruct((m, d), F32), jax.ShapeDtypeStruct((m, d), F32)],
        scratch_shapes=[pltpu.VMEM((d // tn, tm, tn), F32), pltpu.VMEM((tm, d), F32),
                        pltpu.VMEM((tm, d), F32), pltpu.VMEM((tm, d), F32),
                        pltpu.SemaphoreType.DMA((3,))],
        compiler_params=_cparams(2),
        name="out_projection",
    )(merged, w_out, x, post_mix, g1, pre_ffn, sh2, sc2)


def _split_bf16(x):
    hi = x.astype(BF16)
    return hi, (x - hi.astype(F32)).astype(BF16)


def _router_kernel(x_ref, wr_ref, aff_ref):
    n_e = wr_ref.shape[0]
    b_hi, b_lo = _split_bf16(x_ref[0])
    w_hi, w_lo = _split_bf16(wr_ref[...])
    p1 = _dot_nt(jnp.concatenate([w_hi, w_lo], axis=0), b_hi)
    logits = p1[:n_e] + p1[n_e:] + _dot_nt(w_hi, b_lo)
    z = jnp.exp(logits - jnp.max(logits, axis=0, keepdims=True))
    aff_ref[0] = z / jnp.sum(z, axis=0, keepdims=True)


def _router(b_lat, w_router_t):
    b, n, d = b_lat.shape
    n_e = w_router_t.shape[0]
    t = min(n, 512)
    return pl.pallas_call(
        _router_kernel,
        grid=(b, n // t),
        in_specs=[pl.BlockSpec((1, t, d), lambda bi, i: (bi, i, 0)),
                  pl.BlockSpec((n_e, d), lambda bi, i: (0, 0))],
        out_specs=pl.BlockSpec((1, n_e, t), lambda bi, i: (bi, 0, i)),
        out_shape=jax.ShapeDtypeStruct((b, n_e, n), F32),
        compiler_params=_cparams(2),
        name="router_affinity",
    )(b_lat, w_router_t)


def _prefix_count(x01, tri):
    rows, n = x01.shape
    off = jnp.zeros((rows, 1), F32)
    outs = []
    for c in range(n // LANES):
        y = _dot(x01[:, c * LANES:(c + 1) * LANES].astype(BF16), tri) + off
        outs.append(y)
        off = y[:, LANES - 1:LANES]
    return jnp.concatenate(outs, axis=1)


def _select_kernel(cap, tile_shift, aff_ref, slot_ref, gate_ref, idx_ref, lo_ref):
    aff = aff_ref[0]
    n_e, n = aff.shape
    r = lax.broadcasted_iota(jnp.int32, (LANES, LANES), 0)
    c = lax.broadcasted_iota(jnp.int32, (LANES, LANES), 1)
    tri = (r <= c).astype(BF16)
    strict = (r < c).astype(BF16)

    def as_float(pattern):
        return pltpu.bitcast(pattern, F32)

    def body(i, ans):
        cand = ans | lax.shift_left(jnp.int32(1), 30 - i)
        cnt = jnp.sum((aff >= as_float(cand)).astype(F32), axis=1, keepdims=True)
        return jnp.where(cnt >= cap, cand, ans)

    kth = lax.fori_loop(0, 31, body, jnp.zeros((n_e, 1), jnp.int32))
    above = aff >= as_float(kth + 1)
    tied = (aff >= as_float(kth)) & jnp.logical_not(above)
    need = cap - jnp.sum(above.astype(F32), axis=1, keepdims=True)
    tied_rank = _prefix_count(tied.astype(F32), tri)
    chosen = above | (tied & (tied_rank <= need))
    slot = _prefix_count(chosen.astype(F32), tri) - 1.0
    slot = jnp.where(chosen, slot, -1.0).astype(jnp.int32)
    slot_ref[0] = slot

    j = lax.broadcasted_iota(jnp.int32, (cap, 1), 0)
    tok = lax.broadcasted_iota(jnp.int32, (1, n), 1).astype(F32)
    for e in range(n_e):
        hit = slot[e:e + 1, :] == j
        gate_ref[0, e] = jnp.sum(jnp.where(hit, aff[e:e + 1, :], 0.0), axis=1, keepdims=True)
        idx_ref[0, e] = jnp.sum(jnp.where(hit, tok, 0.0), axis=1, keepdims=True).astype(jnp.int32)

    t_of = lax.shift_right_logical(lax.broadcasted_iota(jnp.int32, (n, LANES), 0), tile_shift)
    in_tile = (t_of == lax.broadcasted_iota(jnp.int32, (n, LANES), 1)).astype(BF16)
    per_tile = _dot(chosen.astype(BF16), in_tile)
    lo_ref[0] = _dot(per_tile.astype(BF16), strict).astype(jnp.int32)


def _select(aff, cap, tile):
    b, n_e, n = aff.shape
    tile_shift = tile.bit_length() - 1
    assert tile == 1 << tile_shift and tile <= 256 and n // tile < LANES
    return pl.pallas_call(
        functools.partial(_select_kernel, cap, tile_shift),
        grid=(b,),
        in_specs=[pl.BlockSpec((1, n_e, n), lambda bi: (bi, 0, 0))],
        out_specs=[pl.BlockSpec((1, n_e, n), lambda bi: (bi, 0, 0)),
                   pl.BlockSpec((1, n_e, cap, 1), lambda bi: (bi, 0, 0, 0)),
                   pl.BlockSpec((1, n_e, cap, 1), lambda bi: (bi, 0, 0, 0)),
                   pl.BlockSpec((1, n_e, LANES), lambda bi: (bi, 0, 0))],
        out_shape=[jax.ShapeDtypeStruct((b, n_e, n), jnp.int32),
                   jax.ShapeDtypeStruct((b, n_e, cap, 1), F32),
                   jax.ShapeDtypeStruct((b, n_e, cap, 1), jnp.int32),
                   jax.ShapeDtypeStruct((b, n_e, LANES), jnp.int32)],
        compiler_params=_cparams(1),
        name="expert_choice_select",
    )(aff)


def _gather_kernel(n_tok, idx_ref, idx_next_ref, src_ref, o_ref, buf, sems):
    cap = buf.shape[1]
    e, b = pl.program_id(0), pl.program_id(1)
    n_b = pl.num_programs(1)
    step = e * n_b + b
    cur = lax.rem(step, 2)

    def row_copy(idx, sample, j, half):
        return pltpu.make_async_copy(src_ref.at[pl.ds(sample * n_tok + idx[0, 0, j], 1), :],
                                     buf.at[half, pl.ds(j, 1), :], sems.at[half])

    def start_rows(idx, sample, half):
        def issue(j, carry):
            row_copy(idx, sample, j, half).start()
            return carry
        lax.fori_loop(0, cap, issue, 0, unroll=8)

    @pl.when(step == 0)
    def _():
        start_rows(idx_ref, b, cur)

    @pl.when(step < pl.num_programs(0) * n_b - 1)
    def _():
        start_rows(idx_next_ref, lax.rem(b + 1, n_b), 1 - cur)

    def drain(j, carry):
        row_copy(idx_ref, b, j, cur).wait()
        return carry

    lax.fori_loop(0, cap, drain, 0, unroll=8)
    o_ref[0, 0] = buf[cur].astype(BF16)


def _gather(idx, b_lat, n_tok):
    cap = idx.shape[-1]
    d = b_lat.shape[-1]
    b = b_lat.shape[0] // n_tok
    n_e = idx.shape[0] // b

    def idx_row(e, bi):
        return bi * n_e + e

    def next_idx_row(e, bi):
        wrap = (bi + 1) // b
        return idx_row(jnp.minimum(e + wrap, n_e - 1), (bi + 1) % b)

    return pl.pallas_call(
        functools.partial(_gather_kernel, n_tok),
        grid=(n_e, b),
        in_specs=[pl.BlockSpec((1, 1, cap), lambda e, bi: (idx_row(e, bi), 0, 0), memory_space=pltpu.SMEM),
                  pl.BlockSpec((1, 1, cap), lambda e, bi: (next_idx_row(e, bi), 0, 0),
                               memory_space=pltpu.SMEM),
                  pl.BlockSpec(memory_space=pl.ANY)],
        out_specs=pl.BlockSpec((1, 1, cap, d), lambda e, bi: (e, bi, 0, 0)),
        out_shape=jax.ShapeDtypeStruct((n_e, b, cap, d), BF16),
        scratch_shapes=[pltpu.VMEM((2, cap, d), F32), pltpu.SemaphoreType.DMA((2,))],
        compiler_params=_cparams(2),
        name="moe_gather",
    )(idx, idx, b_lat)


def _expert_up_kernel(x_hbm, wg_ref, wu_ref, o_ref, x_buf, sems):
    e, j = pl.program_id(0), pl.program_id(1)
    last_e, last_j = pl.num_programs(0) - 1, pl.num_programs(1) - 1
    n_b = x_buf.shape[0]

    def x_copy(expert, bi):
        return pltpu.make_async_copy(x_hbm.at[expert, bi], x_buf.at[bi], sems.at[bi])

    @pl.when((e == 0) & (j == 0))
    def _():
        for bi in range(n_b):
            x_copy(0, bi).start()

    @pl.when(j == 0)
    def _():
        for bi in range(n_b):
            x_copy(e, bi).wait()

    def compute(prefetch_next):
        wg = wg_ref[0].astype(BF16)
        wu = wu_ref[0].astype(BF16)
        for bi in range(n_b):
            x = x_buf[bi]
            g = _dot(x, wg)
            o_ref[0, bi] = (g * jax.nn.sigmoid(g) * _dot(x, wu)).astype(BF16)
            if prefetch_next:
                x_copy(e + 1, bi).start()

    refill = (j == last_j) & (e < last_e)

    @pl.when(refill)
    def _():
        compute(True)

    @pl.when(jnp.logical_not(refill))
    def _():
        compute(False)


def _expert_up(xin, w_gate, w_up):
    n_e, b, cap, d = xin.shape
    f = w_gate.shape[-1]
    tn = min(f, 256)
    return pl.pallas_call(
        _expert_up_kernel,
        grid=(n_e, f // tn),
        in_specs=[pl.BlockSpec(memory_space=pl.ANY),
                  pl.BlockSpec((1, d, tn), lambda e, j: (e, 0, j)),
                  pl.BlockSpec((1, d, tn), lambda e, j: (e, 0, j))],
        out_specs=pl.BlockSpec((1, b, cap, tn), lambda e, j: (e, 0, 0, j)),
        out_shape=jax.ShapeDtypeStruct((n_e, b, cap, f), BF16),
        scratch_shapes=[pltpu.VMEM((b, cap, d), BF16), pltpu.SemaphoreType.DMA((b,))],
        compiler_params=_cparams(2),
        name="expert_up",
    )(xin, w_gate, w_up)


def _expert_down_kernel(h_ref, w_ref, g_ref, o_ref):
    w = w_ref[0].astype(BF16)
    for bi in range(h_ref.shape[1]):
        o_ref[0, bi] = (_dot(h_ref[0, bi], w) * g_ref[bi, 0]).astype(BF16)


def _expert_down(hid, w_down, gate):
    n_e, b, cap, f = hid.shape
    d = w_down.shape[-1]
    tn = min(d, 512)
    return pl.pallas_call(
        _expert_down_kernel,
        grid=(n_e, d // tn),
        in_specs=[pl.BlockSpec((1, b, cap, f), lambda e, j: (e, 0, 0, 0)),
                  pl.BlockSpec((1, f, tn), lambda e, j: (e, 0, j)),
                  pl.BlockSpec((b, 1, cap, 1), lambda e, j: (0, e, 0, 0))],
        out_specs=pl.BlockSpec((1, b, cap, tn), lambda e, j: (e, 0, 0, j)),
        out_shape=jax.ShapeDtypeStruct((n_e, b, cap, d), BF16),
        compiler_params=_cparams(2),
        name="expert_down",
    )(hid, w_down, gate)


def _combine_kernel(n_e, n_tiles, lo_ref, slot_ref, y_ref, h_ref, pfn_ref, g2_ref, o_ref, buf, acc, sems):
    bi, i, grp = pl.program_id(0), pl.program_id(1), pl.program_id(2)
    n_groups = pl.num_programs(2)
    group = slot_ref.shape[1]
    piece, chunk = MOE_PIECE_ROWS, MOE_CHUNK_ROWS
    shift = piece.bit_length() - 1
    step = (bi * n_tiles + i) * n_groups + grp
    cur = lax.rem(step, 2)

    def piece_ranges(sample, tile, g):
        out = []
        for k in range(group):
            e = g * group + k
            row = (sample * n_e + e) * (n_tiles + 1) + tile
            lo, hi = lo_ref[row], lo_ref[row + 1]
            first = lax.shift_right_logical(lo, shift)
            count = jnp.where(hi > lo, lax.shift_right_logical(hi + (piece - 1), shift) - first, 0)
            out.append((e, first, count))
        return out

    def piece_copy(sample, e, src_piece, half, dst_row):
        return pltpu.make_async_copy(
            y_ref.at[e, sample, pl.ds(pl.multiple_of(src_piece * piece, piece), piece), :],
            buf.at[half, pl.ds(pl.multiple_of(dst_row, piece), piece), :], sems.at[half])

    def start_fetch(sample, tile, g, half):
        filled = jnp.int32(0)
        for e, first, count in piece_ranges(sample, tile, g):
            def issue(p, carry, e=e, first=first, filled=filled):
                piece_copy(sample, e, first + p, half, filled + p * piece).start()
                return carry
            lax.fori_loop(0, count, issue, 0)
            filled = filled + count * piece

    @pl.when(step == 0)
    def _():
        buf[...] = jnp.zeros_like(buf)
        start_fetch(bi, i, grp, cur)

    @pl.when(step < pl.num_programs(0) * n_tiles * n_groups - 1)
    def _():
        wrap_g = grp == n_groups - 1
        nxt_i = jnp.where(wrap_g, i + 1, i)
        wrap_i = nxt_i == n_tiles
        start_fetch(jnp.where(wrap_i, bi + 1, bi), jnp.where(wrap_i, 0, nxt_i),
                    jnp.where(wrap_g, 0, grp + 1), 1 - cur)

    filled = jnp.int32(0)
    shifts = []
    for _, first, count in piece_ranges(bi, i, grp):
        shifts.append(filled - first * piece)
        filled = filled + count * piece

    def drain(p, carry):
        piece_copy(bi, 0, 0, cur, 0).wait()
        return carry

    lax.fori_loop(0, lax.shift_right_logical(filled, shift), drain, 0)

    slot = slot_ref[0]
    target = jnp.concatenate(
        [jnp.where(slot[k:k + 1, :] >= 0, slot[k:k + 1, :] + shifts[k], -1) for k in range(group)], axis=0)

    @pl.when(grp == 0)
    def _():
        acc[...] = jnp.zeros_like(acc)

    def add_chunk(c, carry):
        r0 = pl.multiple_of(c * chunk, chunk)
        rows = lax.broadcasted_iota(jnp.int32, (chunk, 1), 0) + r0
        hit = target[0:1, :] == rows
        for k in range(1, group):
            hit = hit | (target[k:k + 1, :] == rows)
        acc[...] += _dot_tn(hit.astype(BF16), buf[cur, pl.ds(r0, chunk), :])
        return carry

    lax.fori_loop(0, lax.shift_right_logical(filled + (chunk - 1), chunk.bit_length() - 1), add_chunk, 0)

    @pl.when(grp == n_groups - 1)
    def _():
        o_ref[0] = h_ref[0] + g2_ref[0] * _rms(acc[...], pfn_ref[...])


def _combine(lo, slot, y, h, post_ffn, g2, tile):
    b, n_e, n = slot.shape
    cap, d = y.shape[2], y.shape[3]
    n_tiles = n // tile
    group = MOE_COMBINE_GROUP
    assert cap % MOE_PIECE_ROWS == 0 and MOE_CHUNK_ROWS % MOE_PIECE_ROWS == 0 and n_e % group == 0
    buf_rows = group * (tile + 2 * MOE_PIECE_ROWS)
    buf_rows = -(-buf_rows // MOE_CHUNK_ROWS) * MOE_CHUNK_ROWS
    return pl.pallas_call(
        functools.partial(_combine_kernel, n_e, n_tiles),
        grid_spec=pltpu.PrefetchScalarGridSpec(
            num_scalar_prefetch=1,
            grid=(b, n_tiles, n_e // group),
            in_specs=[pl.BlockSpec((1, group, tile), lambda bi, i, g, lo_ref: (bi, g, i)),
                      pl.BlockSpec(memory_space=pl.ANY),
                      pl.BlockSpec((1, tile, d), lambda bi, i, g, lo_ref: (bi, i, 0)),
                      pl.BlockSpec((1, d), lambda bi, i, g, lo_ref: (0, 0)),
                      pl.BlockSpec((1, 1, d), lambda bi, i, g, lo_ref: (bi, 0, 0))],
            out_specs=pl.BlockSpec((1, tile, d), lambda bi, i, g, lo_ref: (bi, i, 0)),
            scratch_shapes=[pltpu.VMEM((2, buf_rows, d), BF16), pltpu.VMEM((tile, d), F32),
                            pltpu.SemaphoreType.DMA((2,))]),
        out_shape=jax.ShapeDtypeStruct((b, n, d), F32),
        compiler_params=_cparams(3),
        name="moe_combine",
    )(lo, slot, y, h, post_ffn, g2)


def _rope_tables(n):
    t = jnp.arange(n, dtype=jnp.int32)
    pos_row = (t // GRID_W).astype(F32)
    pos_col = (t % GRID_W).astype(F32)
    n_freq = HEAD_DIM // 4
    inv_freq = ROPE_THETA ** (-jnp.arange(n_freq, dtype=F32) / n_freq)
    ang_r = pos_row[:, None] * inv_freq
    ang_c = pos_col[:, None] * inv_freq
    cos = jnp.concatenate([jnp.cos(ang_r)] * 2 + [jnp.cos(ang_c)] * 2, axis=1)
    sin = jnp.concatenate([-jnp.sin(ang_r), jnp.sin(ang_r), -jnp.sin(ang_c), jnp.sin(ang_c)], axis=1)
    return cos, sin


def kernel(x, c, ctx, c_ctx, w_mod, b_mod, pre_mix_norm, post_mix_norm, pre_ffn_norm, post_ffn_norm,
           w_in, q_norm, k_norm, sgu_norm, sgu_w, sgu_b, w_br_attn, w_br_sgu, w_out,
           w_router, w_gate, w_up, w_down):
    b, n, d = x.shape
    n_ctx = ctx.shape[1]
    assert w_mod.shape[0] == 1, "single-layer block"
    n_e = w_router.shape[-1]
    cap = EC_CAPACITY_FACTOR * n // n_e

    rows = -(-(b + 1) // SUBLANES) * SUBLANES
    cc = jnp.zeros((rows, d), F32).at[:b].set(c).at[b].set(c_ctx)
    mod = _mod_vectors(cc, w_mod[0], b_mod)
    sh1, sc1, g1, sh2, sc2, g2 = [mod[:, i * d:(i + 1) * d].reshape(rows, 1, d) for i in range(N_MOD)]

    a_lat = _norm_mod(x, pre_mix_norm, sh1, sc1, lambda bi: bi).reshape(b * n, d)
    a_ctx = _norm_mod(ctx, pre_mix_norm, sh1, sc1, lambda bi: b).reshape(b * n_ctx, d)

    w_in_b = w_in[0].astype(BF16)
    cos, sin = _rope_tables(n)
    lay = _proj_layout(d)
    proj = _in_proj(a_lat, w_in_b, cos, sin, k_norm, sgu_norm, n, d)
    k_c, v_c = _ctx_kv(a_ctx, w_in_b, k_norm, d)

    attn = _attention(proj.reshape(b, n, lay["end"]), k_c.reshape(b, n_ctx, KV_WIDTH),
                      v_c.reshape(b, n_ctx, KV_WIDTH), q_norm, cos, sin, lay)
    merged = _merge(attn.reshape(b * n, ATTN_WIDTH), proj, sgu_w[0], sgu_b[0].T,
                    w_br_attn[0].astype(BF16), w_br_sgu[0].astype(BF16), n, d, lay)
    h1, b_lat = _out_proj(merged, w_out[0].astype(BF16), x.reshape(b * n, d), post_mix_norm, g1,
                          pre_ffn_norm, sh2, sc2, n, d)
    h1 = h1.reshape(b, n, d)

    tile = min(n, MOE_TOKEN_TILE)
    aff = _router(b_lat.reshape(b, n, d), w_router[0].T)
    slot, gate, idx, lo = _select(aff, cap, tile)
    xin = _gather(idx.reshape(b * n_e, 1, cap), b_lat, n)
    hid = _expert_up(xin, w_gate[0], w_up[0])
    y = _expert_down(hid, w_down[0], gate)
    lo_flat = lo[:, :, :n // tile + 1].reshape(-1)
    return _combine(lo_flat, slot, y, h1, post_ffn_norm, g2, tile)
```

```python
import functools
import math

import jax
import jax.numpy as jnp
from jax import lax
from jax.experimental import pallas as pl
from jax.experimental.pallas import tpu as pltpu

F32 = jnp.float32
BF16 = jnp.bfloat16

GRID_W = 64
N_HEADS = 16
N_KV_HEADS = 4
HEAD_DIM = 128
GQA_GROUP = N_HEADS // N_KV_HEADS
ATTN_WIDTH = N_HEADS * HEAD_DIM
KV_WIDTH = N_KV_HEADS * HEAD_DIM
ROPE_THETA = 10000.0
SGU_CHUNK = 128
SGU_GROUPS = 8
EC_CAPACITY_FACTOR = 2
NORM_EPS = 1e-6
N_MOD = 6
QK_PRESCALE = HEAD_DIM ** -0.5 * math.log2(math.e)
ATTN_KEY_CHUNK = 512
ATTN_BLOCK_ROWS = 256
MOE_PIECE_ROWS = 16
MOE_CHUNK_ROWS = 256
MOE_TOKEN_TILE = 256
MOE_COMBINE_GROUP = 4

LANES = 128
SUBLANES = 8
VMEM_BUDGET_BYTES = 56 * 1024 * 1024


def _cparams(n_axes, vmem=VMEM_BUDGET_BYTES):
    return pltpu.CompilerParams(
        dimension_semantics=("arbitrary",) * n_axes, vmem_limit_bytes=vmem)


def _rms(x, gain):
    return x * lax.rsqrt(jnp.mean(x * x, axis=-1, keepdims=True) + NORM_EPS) * gain


def _dot(a, b):
    return jnp.dot(a, b, preferred_element_type=F32)


def _dot_nt(a, b):
    return lax.dot_general(a, b, (((1,), (1,)), ((), ())), preferred_element_type=F32)


def _dot_tn(a, b):
    return lax.dot_general(a, b, (((0,), (0,)), ((), ())), preferred_element_type=F32)


def _mod_kernel(c_ref, w_ref, b_ref, o_ref):
    c = c_ref[...]
    s = (c * jax.nn.sigmoid(c)).astype(BF16)
    o_ref[...] = _dot(s, w_ref[...].astype(BF16)) + b_ref[...]


def _mod_vectors(cc, w_mod, b_mod):
    rows, d = cc.shape
    n = w_mod.shape[1]
    tn = min(n, 512)
    return pl.pallas_call(
        _mod_kernel,
        grid=(n // tn,),
        in_specs=[pl.BlockSpec((rows, d), lambda j: (0, 0)),
                  pl.BlockSpec((d, tn), lambda j: (0, j)),
                  pl.BlockSpec((1, tn), lambda j: (0, j))],
        out_specs=pl.BlockSpec((rows, tn), lambda j: (0, j)),
        out_shape=jax.ShapeDtypeStruct((rows, n), F32),
        compiler_params=_cparams(1),
        name="mod_vectors",
    )(cc, w_mod, b_mod)


def _norm_mod_kernel(x_ref, g_ref, sh_ref, sc_ref, o_ref):
    x = x_ref[0]
    o_ref[0] = (_rms(x, g_ref[...]) * (1.0 + sc_ref[0]) + sh_ref[0]).astype(BF16)


def _norm_mod(x, gain, shift, scale, row_of_batch):
    b, n, d = x.shape
    tm = min(n, 512)
    return pl.pallas_call(
        _norm_mod_kernel,
        grid=(b, n // tm),
        in_specs=[pl.BlockSpec((1, tm, d), lambda bi, i: (bi, i, 0)),
                  pl.BlockSpec((1, d), lambda bi, i: (0, 0)),
                  pl.BlockSpec((1, 1, d), lambda bi, i: (row_of_batch(bi), 0, 0)),
                  pl.BlockSpec((1, 1, d), lambda bi, i: (row_of_batch(bi), 0, 0))],
        out_specs=pl.BlockSpec((1, tm, d), lambda bi, i: (bi, i, 0)),
        out_shape=jax.ShapeDtypeStruct((b, n, d), BF16),
        compiler_params=_cparams(2),
        name="norm_modulate",
    )(x, gain, shift, scale)


def _head_norm_rope(acc, gain, cos, sin, lane_lo):
    outs = []
    for h in range(acc.shape[1] // HEAD_DIM):
        y = _rms(acc[:, h * HEAD_DIM:(h + 1) * HEAD_DIM], gain)
        if cos is not None:
            partner = jnp.where(lane_lo, pltpu.roll(y, HEAD_DIM - HEAD_DIM // 4, axis=1),
                                pltpu.roll(y, HEAD_DIM // 4, axis=1))
            y = y * cos + partner * sin
        outs.append(y)
    return outs[0] if len(outs) == 1 else jnp.concatenate(outs, axis=1)


def _rope_lane_lo():
    lane = lax.broadcasted_iota(jnp.int32, (1, HEAD_DIM), 1)
    return (lane % (HEAD_DIM // 2)) < (HEAD_DIM // 4)


def _in_proj_kernel(kv_tile, s_tiles, group_dim, a_ref, w_ref, cos_ref, sin_ref, kn_ref, sn_ref, o_ref):
    j = pl.program_id(1)
    acc = _dot(a_ref[...], w_ref[...])
    s_lo, s_hi = s_tiles
    plain = (j != kv_tile) & ((j < s_lo) | (j >= s_hi))

    @pl.when(plain)
    def _():
        o_ref[...] = acc.astype(BF16)

    @pl.when(j == kv_tile)
    def _():
        k = _head_norm_rope(acc[:, :KV_WIDTH], kn_ref[...], cos_ref[...], sin_ref[...], _rope_lane_lo())
        o_ref[:, :KV_WIDTH] = k.astype(BF16)
        o_ref[:, KV_WIDTH:] = acc[:, KV_WIDTH:].astype(BF16)

    @pl.when((j >= s_lo) & (j < s_hi))
    def _():
        gain = sn_ref[...]
        for c in range(acc.shape[1] // group_dim):
            cols = slice(c * group_dim, (c + 1) * group_dim)
            o_ref[:, cols] = _rms(jax.nn.gelu(acc[:, cols]), gain[:, cols]).astype(BF16)


def _proj_layout(d):
    sw = d // 2
    names = ("q", "k", "v", "u", "s", "ga", "gb")
    widths = (ATTN_WIDTH, KV_WIDTH, KV_WIDTH, sw, sw, d, d)
    off, out = 0, {}
    for name, wd in zip(names, widths):
        out[name] = off
        off += wd
    out["end"] = off
    return out


def _in_proj(a, w, cos, sin, k_norm, sgu_norm, seq, d):
    m = a.shape[0]
    sw = d // 2
    tm = min(seq, 1024)
    tn = 2 * KV_WIDTH
    lay = _proj_layout(d)
    assert all(off % tn == 0 for name, off in lay.items() if name != "v") and lay["v"] == lay["k"] + KV_WIDTH
    s_lo = lay["s"] // tn
    s_tiles = (s_lo, lay["ga"] // tn)
    pos_tiles = seq // tm
    return pl.pallas_call(
        functools.partial(_in_proj_kernel, lay["k"] // tn, s_tiles, sw // SGU_GROUPS),
        grid=(m // tm, lay["end"] // tn),
        in_specs=[pl.BlockSpec((tm, d), lambda i, j: (i, 0)),
                  pl.BlockSpec((d, tn), lambda i, j: (0, j)),
                  pl.BlockSpec((tm, HEAD_DIM), lambda i, j: (i % pos_tiles, 0)),
                  pl.BlockSpec((tm, HEAD_DIM), lambda i, j: (i % pos_tiles, 0)),
                  pl.BlockSpec((1, HEAD_DIM), lambda i, j: (0, 0)),
                  pl.BlockSpec((1, tn), lambda i, j: (0, jnp.clip(j - s_lo, 0, s_tiles[1] - s_lo - 1)))],
        out_specs=pl.BlockSpec((tm, tn), lambda i, j: (i, j)),
        out_shape=jax.ShapeDtypeStruct((m, lay["end"]), BF16),
        compiler_params=_cparams(2),
        name="in_projection",
    )(a, w, cos, sin, k_norm, sgu_norm)


def _ctx_kv_kernel(a_ref, wk_ref, wv_ref, kn_ref, k_ref, v_ref):
    a = a_ref[...]
    k_ref[...] = _head_norm_rope(_dot(a, wk_ref[...]), kn_ref[...], None, None, None).astype(BF16)
    v_ref[...] = _dot(a, wv_ref[...]).astype(BF16)


def _ctx_kv(a, w, k_norm, d):
    m = a.shape[0]
    tm = min(m, 512)
    kb = ATTN_WIDTH // KV_WIDTH
    return pl.pallas_call(
        _ctx_kv_kernel,
        grid=(m // tm,),
        in_specs=[pl.BlockSpec((tm, d), lambda i: (i, 0)),
                  pl.BlockSpec((d, KV_WIDTH), lambda i: (0, kb)),
                  pl.BlockSpec((d, KV_WIDTH), lambda i: (0, kb + 1)),
                  pl.BlockSpec((1, HEAD_DIM), lambda i: (0, 0))],
        out_specs=[pl.BlockSpec((tm, KV_WIDTH), lambda i: (i, 0))] * 2,
        out_shape=[jax.ShapeDtypeStruct((m, KV_WIDTH), BF16)] * 2,
        compiler_params=_cparams(1),
        name="context_kv",
    )(a, w, w, k_norm)


def _lane_fold(x, op):
    out = x[:, :LANES]
    for c in range(1, x.shape[1] // LANES):
        out = op(out, x[:, c * LANES:(c + 1) * LANES])
    return out


def _attn_kernel(q_ref, kl_ref, vl_ref, kc_ref, vc_ref, qn_ref, cos_ref, sin_ref, o_ref, s_scr):
    tq = q_ref.shape[1]
    n, n_ctx = kl_ref.shape[1], kc_ref.shape[1]
    ck = min(n, ATTN_KEY_CHUNK)
    chunks = [(kl_ref, vl_ref, c * ck, ck, c * ck) for c in range(n // ck)] + [(kc_ref, vc_ref, 0, n_ctx, n)]
    lane_lo = _rope_lane_lo()
    rows_per_block = s_scr.shape[1]
    n_buf = s_scr.shape[0]
    blocks = [(g, r * rows_per_block) for r in range(tq // rows_per_block) for g in range(GQA_GROUP)]

    def scores(t):
        g, r0 = blocks[t]
        rows = slice(r0, r0 + rows_per_block)
        q = q_ref[0, rows, g * HEAD_DIM:(g + 1) * HEAD_DIM].astype(F32)
        q = _head_norm_rope(q, qn_ref[...], cos_ref[rows, :], sin_ref[rows, :], lane_lo)
        q = (q * QK_PRESCALE).astype(BF16)
        mvec = None
        for k_ref, _, k0, w, col in chunks:
            s = _dot_nt(q, k_ref[0, k0:k0 + w, :])
            s_scr[t % n_buf, :, col:col + w] = s
            part = _lane_fold(s, jnp.maximum)
            mvec = part if mvec is None else jnp.maximum(mvec, part)
        return jnp.max(mvec, axis=1, keepdims=True)

    def weighted_values(t, m):
        g, r0 = blocks[t]
        lvec = jnp.zeros((rows_per_block, LANES), F32)
        acc = jnp.zeros((rows_per_block, HEAD_DIM), F32)
        for _, v_ref, k0, w, col in chunks:
            p = jnp.exp2(s_scr[t % n_buf, :, col:col + w] - m)
            lvec = lvec + _lane_fold(p, jnp.add)
            acc = acc + _dot(p.astype(BF16), v_ref[0, k0:k0 + w, :])
        o = acc / jnp.sum(lvec, axis=1, keepdims=True)
        o_ref[0, r0:r0 + rows_per_block, g * HEAD_DIM:(g + 1) * HEAD_DIM] = o.astype(BF16)

    m_prev = None
    for t in range(len(blocks) + 1):
        m_cur = scores(t) if t < len(blocks) else None
        if t > 0:
            weighted_values(t - 1, m_prev)
        m_prev = m_cur


def _attention(proj, k_ctx, v_ctx, q_norm, cos, sin, lay):
    b, n, _ = proj.shape
    n_ctx = k_ctx.shape[1]
    tq = min(n, 2 * ATTN_BLOCK_ROWS)
    qw = GQA_GROUP * HEAD_DIM
    assert n % min(n, ATTN_KEY_CHUNK) == 0
    q0, k0, v0 = lay["q"] // qw, lay["k"] // HEAD_DIM, lay["v"] // HEAD_DIM
    return pl.pallas_call(
        _attn_kernel,
        grid=(b, N_KV_HEADS, n // tq),
        in_specs=[pl.BlockSpec((1, tq, qw), lambda bi, h, i: (bi, i, q0 + h)),
                  pl.BlockSpec((1, n, HEAD_DIM), lambda bi, h, i: (bi, 0, k0 + h)),
                  pl.BlockSpec((1, n, HEAD_DIM), lambda bi, h, i: (bi, 0, v0 + h)),
                  pl.BlockSpec((1, n_ctx, HEAD_DIM), lambda bi, h, i: (bi, 0, h)),
                  pl.BlockSpec((1, n_ctx, HEAD_DIM), lambda bi, h, i: (bi, 0, h)),
                  pl.BlockSpec((1, HEAD_DIM), lambda bi, h, i: (0, 0)),
                  pl.BlockSpec((tq, HEAD_DIM), lambda bi, h, i: (i, 0)),
                  pl.BlockSpec((tq, HEAD_DIM), lambda bi, h, i: (i, 0))],
        out_specs=pl.BlockSpec((1, tq, qw), lambda bi, h, i: (bi, i, h)),
        out_shape=jax.ShapeDtypeStruct((b, n, ATTN_WIDTH), BF16),
        scratch_shapes=[pltpu.VMEM((3, min(tq, ATTN_BLOCK_ROWS), n + n_ctx), F32)],
        compiler_params=_cparams(3),
        name="gqa_attention",
    )(proj, proj, proj, k_ctx, v_ctx, q_norm, cos, sin)


def _merge_kernel(group_dim, attn_ref, u0_ref, u1_ref, s0_ref, s1_ref, sw_ref, sbt_ref, wa_ref, ws_ref,
                  ga_ref, gb_ref, o_ref, sgu_scr):
    j = pl.program_id(1)
    tm = attn_ref.shape[0]
    half = SGU_GROUPS // 2

    @pl.when(j == 0)
    def _():
        for g in range(SGU_GROUPS):
            u_ref, s_ref = (u0_ref, s0_ref) if g < half else (u1_ref, s1_ref)
            wg = sw_ref[g].astype(BF16)
            bias = sbt_ref[:, g:g + 1]
            cols = slice((g % half) * group_dim, (g % half + 1) * group_dim)
            out_cols = slice(g * group_dim, (g + 1) * group_dim)
            for c in range(tm // SGU_CHUNK):
                rows = slice(c * SGU_CHUNK, (c + 1) * SGU_CHUNK)
                mixed = _dot(wg, s_ref[rows, cols]) + bias
                sgu_scr[rows, out_cols] = (jax.nn.gelu(u_ref[rows, cols].astype(F32)) * mixed).astype(BF16)

    ya = _dot(attn_ref[...], wa_ref[...])
    ys = _dot(sgu_scr[...], ws_ref[...])
    gate_a = jax.nn.sigmoid(ga_ref[...].astype(F32))
    gate_b = jax.nn.sigmoid(gb_ref[...].astype(F32))
    o_ref[...] = (gate_a * ya + gate_b * ys).astype(BF16)


def _merge(attn, proj, sgu_w, sgu_bt, w_br_attn, w_br_sgu, seq, d, lay):
    m = attn.shape[0]
    sw = d // 2
    hw = sw // 2
    tm = min(seq, 512)
    tn = 1024
    assert lay["u"] % hw == 0 and lay["s"] % hw == 0 and lay["ga"] % tn == 0 and lay["gb"] % tn == 0
    u0, s0, ga0, gb0 = lay["u"] // hw, lay["s"] // hw, lay["ga"] // tn, lay["gb"] // tn
    return pl.pallas_call(
        functools.partial(_merge_kernel, sw // SGU_GROUPS),
        grid=(m // tm, d // tn),
        in_specs=[pl.BlockSpec((tm, ATTN_WIDTH), lambda i, j: (i, 0)),
                  pl.BlockSpec((tm, hw), lambda i, j: (i, u0)),
                  pl.BlockSpec((tm, hw), lambda i, j: (i, u0 + 1)),
                  pl.BlockSpec((tm, hw), lambda i, j: (i, s0)),
                  pl.BlockSpec((tm, hw), lambda i, j: (i, s0 + 1)),
                  pl.BlockSpec((SGU_GROUPS, SGU_CHUNK, SGU_CHUNK), lambda i, j: (0, 0, 0)),
                  pl.BlockSpec((SGU_CHUNK, SGU_GROUPS), lambda i, j: (0, 0)),
                  pl.BlockSpec((ATTN_WIDTH, tn), lambda i, j: (0, j)),
                  pl.BlockSpec((sw, tn), lambda i, j: (0, j)),
                  pl.BlockSpec((tm, tn), lambda i, j: (i, ga0 + j)),
                  pl.BlockSpec((tm, tn), lambda i, j: (i, gb0 + j))],
        out_specs=pl.BlockSpec((tm, tn), lambda i, j: (i, j)),
        out_shape=jax.ShapeDtypeStruct((m, d), BF16),
        scratch_shapes=[pltpu.VMEM((tm, sw), BF16)],
        compiler_params=_cparams(2),
        name="sgu_branch_merge",
    )(attn, proj, proj, proj, proj, sgu_w, sgu_bt, w_br_attn, w_br_sgu, proj, proj)


def _out_proj_kernel(m_ref, w_ref, x_hbm, pmn_ref, g1_ref, pfn_ref, sh2_ref, sc2_ref,
                     h_hbm, b_hbm, y_scr, x_buf, h_buf, b_buf, sems):
    i, j = pl.program_id(0), pl.program_id(1)
    last_i, last_j = pl.num_programs(0) - 1, pl.num_programs(1) - 1
    tm = m_ref.shape[0]

    def rows(t):
        return pl.ds(pl.multiple_of(t * tm, tm), tm)

    def x_read():
        return pltpu.make_async_copy(x_hbm.at[rows(i), :], x_buf, sems.at[0])

    def h_write(t):
        return pltpu.make_async_copy(h_buf, h_hbm.at[rows(t), :], sems.at[1])

    def b_write(t):
        return pltpu.make_async_copy(b_buf, b_hbm.at[rows(t), :], sems.at[2])

    @pl.when(j == 0)
    def _():
        x_read().start()

    y_scr[j] = _dot(m_ref[...], w_ref[...])

    @pl.when(j == last_j)
    def _():
        x_read().wait()

        @pl.when(i > 0)
        def _():
            h_write(i - 1).wait()
            b_write(i - 1).wait()

        n_t, _, tn = y_scr.shape
        d = n_t * tn
        col = lambda t: slice(t * tn, (t + 1) * tn)
        ssq = sum(jnp.sum(y_scr[t] * y_scr[t], axis=1, keepdims=True) for t in range(n_t))
        y_inv = lax.rsqrt(ssq / d + NORM_EPS)
        ssq = jnp.zeros_like(ssq)
        for t in range(n_t):
            gain = g1_ref[0, :, col(t)] * pmn_ref[:, col(t)]
            h = x_buf[:, col(t)] + (y_scr[t] * y_inv) * gain
            h_buf[:, col(t)] = h
            ssq = ssq + jnp.sum(h * h, axis=1, keepdims=True)
        h_inv = lax.rsqrt(ssq / d + NORM_EPS)
        for t in range(n_t):
            gain = pfn_ref[:, col(t)] * (1.0 + sc2_ref[0, :, col(t)])
            b_buf[:, col(t)] = (h_buf[:, col(t)] * h_inv) * gain + sh2_ref[0, :, col(t)]
        h_write(i).start()
        b_write(i).start()

        @pl.when(i == last_i)
        def _():
            h_write(i).wait()
            b_write(i).wait()


def _out_proj(merged, w_out, x, post_mix, g1, pre_ffn, sh2, sc2, seq, d):
    m = merged.shape[0]
    tm = min(seq, 512)
    tn = 512
    tiles_per_batch = seq // tm
    row = lambda i, j: (i // tiles_per_batch, 0, 0)
    any_spec = pl.BlockSpec(memory_space=pl.ANY)
    return pl.pallas_call(
        _out_proj_kernel,
        grid=(m // tm, d // tn),
        in_specs=[pl.BlockSpec((tm, d), lambda i, j: (i, 0)),
                  pl.BlockSpec((d, tn), lambda i, j: (0, j)),
                  any_spec,
                  pl.BlockSpec((1, d), lambda i, j: (0, 0)),
                  pl.BlockSpec((1, 1, d), row),
                  pl.BlockSpec((1, d), lambda i, j: (0, 0)),
                  pl.BlockSpec((1, 1, d), row),
                  pl.BlockSpec((1, 1, d), row)],
        out_specs=[any_spec, any_spec],
        out_shape=[jax.ShapeDtypeStruct((m, d), F32), jax.ShapeDtypeStruct((m, d), F32)],
        scratch_shapes=[pltpu.VMEM((d // tn, tm, tn), F32), pltpu.VMEM((tm, d), F32),
                        pltpu.VMEM((tm, d), F32), pltpu.VMEM((tm, d), F32),
                        pltpu.SemaphoreType.DMA((3,))],
        compiler_params=_cparams(2),
        name="out_projection",
    )(merged, w_out, x, post_mix, g1, pre_ffn, sh2, sc2)


def _split_bf16(x):
    hi = x.astype(BF16)
    return hi, (x - hi.astype(F32)).astype(BF16)


def _router_kernel(x_ref, wr_ref, aff_ref):
    n_e = wr_ref.shape[0]
    b_hi, b_lo = _split_bf16(x_ref[0])
    w_hi, w_lo = _split_bf16(wr_ref[...])
    p1 = _dot_nt(jnp.concatenate([w_hi, w_lo], axis=0), b_hi)
    logits = p1[:n_e] + p1[n_e:] + _dot_nt(w_hi, b_lo)
    z = jnp.exp(logits - jnp.max(logits, axis=0, keepdims=True))
    aff_ref[0] = z / jnp.sum(z, axis=0, keepdims=True)


def _router(b_lat, w_router_t):
    b, n, d = b_lat.shape
    n_e = w_router_t.shape[0]
    t = min(n, 512)
    return pl.pallas_call(
        _router_kernel,
        grid=(b, n // t),
        in_specs=[pl.BlockSpec((1, t, d), lambda bi, i: (bi, i, 0)),
                  pl.BlockSpec((n_e, d), lambda bi, i: (0, 0))],
        out_specs=pl.BlockSpec((1, n_e, t), lambda bi, i: (bi, 0, i)),
        out_shape=jax.ShapeDtypeStruct((b, n_e, n), F32),
        compiler_params=_cparams(2),
        name="router_affinity",
    )(b_lat, w_router_t)


def _prefix_count(x01, tri):
    rows, n = x01.shape
    off = jnp.zeros((rows, 1), F32)
    outs = []
    for c in range(n // LANES):
        y = _dot(x01[:, c * LANES:(c + 1) * LANES].astype(BF16), tri) + off
        outs.append(y)
        off = y[:, LANES - 1:LANES]
    return jnp.concatenate(outs, axis=1)


def _select_kernel(cap, tile_shift, aff_ref, slot_ref, gate_ref, idx_ref, lo_ref):
    aff = aff_ref[0]
    n_e, n = aff.shape
    r = lax.broadcasted_iota(jnp.int32, (LANES, LANES), 0)
    c = lax.broadcasted_iota(jnp.int32, (LANES, LANES), 1)
    tri = (r <= c).astype(BF16)
    strict = (r < c).astype(BF16)

    def as_float(pattern):
        return pltpu.bitcast(pattern, F32)

    def body(i, ans):
        cand = ans | lax.shift_left(jnp.int32(1), 30 - i)
        cnt = jnp.sum((aff >= as_float(cand)).astype(F32), axis=1, keepdims=True)
        return jnp.where(cnt >= cap, cand, ans)

    kth = lax.fori_loop(0, 31, body, jnp.zeros((n_e, 1), jnp.int32))
    above = aff >= as_float(kth + 1)
    tied = (aff >= as_float(kth)) & jnp.logical_not(above)
    need = cap - jnp.sum(above.astype(F32), axis=1, keepdims=True)
    tied_rank = _prefix_count(tied.astype(F32), tri)
    chosen = above | (tied & (tied_rank <= need))
    slot = _prefix_count(chosen.astype(F32), tri) - 1.0
    slot = jnp.where(chosen, slot, -1.0).astype(jnp.int32)
    slot_ref[0] = slot

    j = lax.broadcasted_iota(jnp.int32, (cap, 1), 0)
    tok = lax.broadcasted_iota(jnp.int32, (1, n), 1).astype(F32)
    for e in range(n_e):
        hit = slot[e:e + 1, :] == j
        gate_ref[0, e] = jnp.sum(jnp.where(hit, aff[e:e + 1, :], 0.0), axis=1, keepdims=True)
        idx_ref[0, e] = jnp.sum(jnp.where(hit, tok, 0.0), axis=1, keepdims=True).astype(jnp.int32)

    t_of = lax.shift_right_logical(lax.broadcasted_iota(jnp.int32, (n, LANES), 0), tile_shift)
    in_tile = (t_of == lax.broadcasted_iota(jnp.int32, (n, LANES), 1)).astype(BF16)
    per_tile = _dot(chosen.astype(BF16), in_tile)
    lo_ref[0] = _dot(per_tile.astype(BF16), strict).astype(jnp.int32)


def _select(aff, cap, tile):
    b, n_e, n = aff.shape
    tile_shift = tile.bit_length() - 1
    assert tile == 1 << tile_shift and tile <= 256 and n // tile < LANES
    return pl.pallas_call(
        functools.partial(_select_kernel, cap, tile_shift),
        grid=(b,),
        in_specs=[pl.BlockSpec((1, n_e, n), lambda bi: (bi, 0, 0))],
        out_specs=[pl.BlockSpec((1, n_e, n), lambda bi: (bi, 0, 0)),
                   pl.BlockSpec((1, n_e, cap, 1), lambda bi: (bi, 0, 0, 0)),
                   pl.BlockSpec((1, n_e, cap, 1), lambda bi: (bi, 0, 0, 0)),
                   pl.BlockSpec((1, n_e, LANES), lambda bi: (bi, 0, 0))],
        out_shape=[jax.ShapeDtypeStruct((b, n_e, n), jnp.int32),
                   jax.ShapeDtypeStruct((b, n_e, cap, 1), F32),
                   jax.ShapeDtypeStruct((b, n_e, cap, 1), jnp.int32),
                   jax.ShapeDtypeStruct((b, n_e, LANES), jnp.int32)],
        compiler_params=_cparams(1),
        name="expert_choice_select",
    )(aff)


def _gather_kernel(n_tok, idx_ref, idx_next_ref, src_ref, o_ref, buf, sems):
    cap = buf.shape[1]
    e, b = pl.program_id(0), pl.program_id(1)
    n_b = pl.num_programs(1)
    step = e * n_b + b
    cur = lax.rem(step, 2)

    def row_copy(idx, sample, j, half):
        return pltpu.make_async_copy(src_ref.at[pl.ds(sample * n_tok + idx[0, 0, j], 1), :],
                                     buf.at[half, pl.ds(j, 1), :], sems.at[half])

    def start_rows(idx, sample, half):
        def issue(j, carry):
            row_copy(idx, sample, j, half).start()
            return carry
        lax.fori_loop(0, cap, issue, 0, unroll=8)

    @pl.when(step == 0)
    def _():
        start_rows(idx_ref, b, cur)

    @pl.when(step < pl.num_programs(0) * n_b - 1)
    def _():
        start_rows(idx_next_ref, lax.rem(b + 1, n_b), 1 - cur)

    def drain(j, carry):
        row_copy(idx_ref, b, j, cur).wait()
        return carry

    lax.fori_loop(0, cap, drain, 0, unroll=8)
    o_ref[0, 0] = buf[cur].astype(BF16)


def _gather(idx, b_lat, n_tok):
    cap = idx.shape[-1]
    d = b_lat.shape[-1]
    b = b_lat.shape[0] // n_tok
    n_e = idx.shape[0] // b

    def idx_row(e, bi):
        return bi * n_e + e

    def next_idx_row(e, bi):
        wrap = (bi + 1) // b
        return idx_row(jnp.minimum(e + wrap, n_e - 1), (bi + 1) % b)

    return pl.pallas_call(
        functools.partial(_gather_kernel, n_tok),
        grid=(n_e, b),
        in_specs=[pl.BlockSpec((1, 1, cap), lambda e, bi: (idx_row(e, bi), 0, 0), memory_space=pltpu.SMEM),
                  pl.BlockSpec((1, 1, cap), lambda e, bi: (next_idx_row(e, bi), 0, 0),
                               memory_space=pltpu.SMEM),
                  pl.BlockSpec(memory_space=pl.ANY)],
        out_specs=pl.BlockSpec((1, 1, cap, d), lambda e, bi: (e, bi, 0, 0)),
        out_shape=jax.ShapeDtypeStruct((n_e, b, cap, d), BF16),
        scratch_shapes=[pltpu.VMEM((2, cap, d), F32), pltpu.SemaphoreType.DMA((2,))],
        compiler_params=_cparams(2),
        name="moe_gather",
    )(idx, idx, b_lat)


def _expert_up_kernel(x_hbm, wg_ref, wu_ref, o_ref, x_buf, sems):
    e, j = pl.program_id(0), pl.program_id(1)
    last_e, last_j = pl.num_programs(0) - 1, pl.num_programs(1) - 1
    n_b = x_buf.shape[0]

    def x_copy(expert, bi):
        return pltpu.make_async_copy(x_hbm.at[expert, bi], x_buf.at[bi], sems.at[bi])

    @pl.when((e == 0) & (j == 0))
    def _():
        for bi in range(n_b):
            x_copy(0, bi).start()

    @pl.when(j == 0)
    def _():
        for bi in range(n_b):
            x_copy(e, bi).wait()

    def compute(prefetch_next):
        wg = wg_ref[0].astype(BF16)
        wu = wu_ref[0].astype(BF16)
        for bi in range(n_b):
            x = x_buf[bi]
            g = _dot(x, wg)
            o_ref[0, bi] = (g * jax.nn.sigmoid(g) * _dot(x, wu)).astype(BF16)
            if prefetch_next:
                x_copy(e + 1, bi).start()

    refill = (j == last_j) & (e < last_e)

    @pl.when(refill)
    def _():
        compute(True)

    @pl.when(jnp.logical_not(refill))
    def _():
        compute(False)


def _expert_up(xin, w_gate, w_up):
    n_e, b, cap, d = xin.shape
    f = w_gate.shape[-1]
    tn = min(f, 256)
    return pl.pallas_call(
        _expert_up_kernel,
        grid=(n_e, f // tn),
        in_specs=[pl.BlockSpec(memory_space=pl.ANY),
                  pl.BlockSpec((1, d, tn), lambda e, j: (e, 0, j)),
                  pl.BlockSpec((1, d, tn), lambda e, j: (e, 0, j))],
        out_specs=pl.BlockSpec((1, b, cap, tn), lambda e, j: (e, 0, 0, j)),
        out_shape=jax.ShapeDtypeStruct((n_e, b, cap, f), BF16),
        scratch_shapes=[pltpu.VMEM((b, cap, d), BF16), pltpu.SemaphoreType.DMA((b,))],
        compiler_params=_cparams(2),
        name="expert_up",
    )(xin, w_gate, w_up)


def _expert_down_kernel(h_ref, w_ref, g_ref, o_ref):
    w = w_ref[0].astype(BF16)
    for bi in range(h_ref.shape[1]):
        o_ref[0, bi] = (_dot(h_ref[0, bi], w) * g_ref[bi, 0]).astype(BF16)


def _expert_down(hid, w_down, gate):
    n_e, b, cap, f = hid.shape
    d = w_down.shape[-1]
    tn = min(d, 512)
    return pl.pallas_call(
        _expert_down_kernel,
        grid=(n_e, d // tn),
        in_specs=[pl.BlockSpec((1, b, cap, f), lambda e, j: (e, 0, 0, 0)),
                  pl.BlockSpec((1, f, tn), lambda e, j: (e, 0, j)),
                  pl.BlockSpec((b, 1, cap, 1), lambda e, j: (0, e, 0, 0))],
        out_specs=pl.BlockSpec((1, b, cap, tn), lambda e, j: (e, 0, 0, j)),
        out_shape=jax.ShapeDtypeStruct((n_e, b, cap, d), BF16),
        compiler_params=_cparams(2),
        name="expert_down",
    )(hid, w_down, gate)


def _combine_kernel(n_e, n_tiles, lo_ref, slot_ref, y_ref, h_ref, pfn_ref, g2_ref, o_ref, buf, acc, sems):
    bi, i, grp = pl.program_id(0), pl.program_id(1), pl.program_id(2)
    n_groups = pl.num_programs(2)
    group = slot_ref.shape[2]
    piece, chunk = MOE_PIECE_ROWS, MOE_CHUNK_ROWS
    shift = piece.bit_length() - 1
    step = (bi * n_tiles + i) * n_groups + grp
    cur = lax.rem(step, 2)

    def piece_ranges(sample, tile, g):
        out = []
        for k in range(group):
            e = g * group + k
            row = (sample * n_e + e) * (n_tiles + 1) + tile
            lo, hi = lo_ref[row], lo_ref[row + 1]
            first = lax.shift_right_logical(lo, shift)
            count = jnp.where(hi > lo, lax.shift_right_logical(hi + (piece - 1), shift) - first, 0)
            out.append((e, first, count))
        return out

    def piece_copy(sample, e, src_piece, half, dst_row):
        return pltpu.make_async_copy(
            y_ref.at[e, sample, pl.ds(pl.multiple_of(src_piece * piece, piece), piece), :],
            buf.at[half, pl.ds(pl.multiple_of(dst_row, piece), piece), :], sems.at[half])

    def start_fetch(sample, tile, g, half):
        filled = jnp.int32(0)
        for e, first, count in piece_ranges(sample, tile, g):
            def issue(p, carry, e=e, first=first, filled=filled):
                piece_copy(sample, e, first + p, half, filled + p * piece).start()
                return carry
            lax.fori_loop(0, count, issue, 0)
            filled = filled + count * piece

    @pl.when(step == 0)
    def _():
        buf[...] = jnp.zeros_like(buf)
        start_fetch(bi, i, grp, cur)

    @pl.when(step < pl.num_programs(0) * n_tiles * n_groups - 1)
    def _():
        wrap_g = grp == n_groups - 1
        nxt_i = jnp.where(wrap_g, i + 1, i)
        wrap_i = nxt_i == n_tiles
        start_fetch(jnp.where(wrap_i, bi + 1, bi), jnp.where(wrap_i, 0, nxt_i),
                    jnp.where(wrap_g, 0, grp + 1), 1 - cur)

    filled = jnp.int32(0)
    shifts = []
    for _, first, count in piece_ranges(bi, i, grp):
        shifts.append(filled - first * piece)
        filled = filled + count * piece

    def drain(p, carry):
        piece_copy(bi, 0, 0, cur, 0).wait()
        return carry

    lax.fori_loop(0, lax.shift_right_logical(filled, shift), drain, 0)

    slot = slot_ref[0, 0]
    target = jnp.concatenate(
        [jnp.where(slot[k:k + 1, :] >= 0, slot[k:k + 1, :] + shifts[k], -1) for k in range(group)], axis=0)

    @pl.when(grp == 0)
    def _():
        acc[...] = jnp.zeros_like(acc)

    def add_chunk(c, carry):
        r0 = pl.multiple_of(c * chunk, chunk)
        rows = lax.broadcasted_iota(jnp.int32, (chunk, 1), 0) + r0
        hit = target[0:1, :] == rows
        for k in range(1, group):
            hit = hit | (target[k:k + 1, :] == rows)
        acc[...] += _dot_tn(hit.astype(BF16), buf[cur, pl.ds(r0, chunk), :])
        return carry

    lax.fori_loop(0, lax.shift_right_logical(filled + (chunk - 1), chunk.bit_length() - 1), add_chunk, 0)

    @pl.when(grp == n_groups - 1)
    def _():
        o_ref[0] = h_ref[0] + g2_ref[0] * _rms(acc[...], pfn_ref[...])


def _combine(lo, slot, y, h, post_ffn, g2, tile):
    b, n_e, n = slot.shape
    cap, d = y.shape[2], y.shape[3]
    n_tiles = n // tile
    group = MOE_COMBINE_GROUP
    assert cap % MOE_PIECE_ROWS == 0 and MOE_CHUNK_ROWS % MOE_PIECE_ROWS == 0 and n_e % group == 0
    buf_rows = group * (tile + 2 * MOE_PIECE_ROWS)
    buf_rows = -(-buf_rows // MOE_CHUNK_ROWS) * MOE_CHUNK_ROWS
    return pl.pallas_call(
        functools.partial(_combine_kernel, n_e, n_tiles),
        grid_spec=pltpu.PrefetchScalarGridSpec(
            num_scalar_prefetch=1,
            grid=(b, n_tiles, n_e // group),
            in_specs=[pl.BlockSpec((1, 1, group, tile), lambda bi, i, g, lo_ref: (bi, g, 0, i)),
                      pl.BlockSpec(memory_space=pl.ANY),
                      pl.BlockSpec((1, tile, d), lambda bi, i, g, lo_ref: (bi, i, 0)),
                      pl.BlockSpec((1, d), lambda bi, i, g, lo_ref: (0, 0)),
                      pl.BlockSpec((1, 1, d), lambda bi, i, g, lo_ref: (bi, 0, 0))],
            out_specs=pl.BlockSpec((1, tile, d), lambda bi, i, g, lo_ref: (bi, i, 0)),
            scratch_shapes=[pltpu.VMEM((2, buf_rows, d), BF16), pltpu.VMEM((tile, d), F32),
                            pltpu.SemaphoreType.DMA((2,))]),
        out_shape=jax.ShapeDtypeStruct((b, n, d), F32),
        compiler_params=_cparams(3),
        name="moe_combine",
    )(lo, slot.reshape(b, n_e // group, group, n), y, h, post_ffn, g2)


def _rope_tables(n):
    t = jnp.arange(n, dtype=jnp.int32)
    pos_row = (t // GRID_W).astype(F32)
    pos_col = (t % GRID_W).astype(F32)
    n_freq = HEAD_DIM // 4
    inv_freq = ROPE_THETA ** (-jnp.arange(n_freq, dtype=F32) / n_freq)
    ang_r = pos_row[:, None] * inv_freq
    ang_c = pos_col[:, None] * inv_freq
    cos = jnp.concatenate([jnp.cos(ang_r)] * 2 + [jnp.cos(ang_c)] * 2, axis=1)
    sin = jnp.concatenate([-jnp.sin(ang_r), jnp.sin(ang_r), -jnp.sin(ang_c), jnp.sin(ang_c)], axis=1)
    return cos, sin


def kernel(x, c, ctx, c_ctx, w_mod, b_mod, pre_mix_norm, post_mix_norm, pre_ffn_norm, post_ffn_norm,
           w_in, q_norm, k_norm, sgu_norm, sgu_w, sgu_b, w_br_attn, w_br_sgu, w_out,
           w_router, w_gate, w_up, w_down):
    b, n, d = x.shape
    n_ctx = ctx.shape[1]
    assert w_mod.shape[0] == 1, "single-layer block"
    n_e = w_router.shape[-1]
    cap = EC_CAPACITY_FACTOR * n // n_e

    rows = -(-(b + 1) // SUBLANES) * SUBLANES
    cc = jnp.zeros((rows, d), F32).at[:b].set(c).at[b].set(c_ctx)
    mod = _mod_vectors(cc, w_mod[0], b_mod)
    sh1, sc1, g1, sh2, sc2, g2 = [mod[:, i * d:(i + 1) * d].reshape(rows, 1, d) for i in range(N_MOD)]

    a_lat = _norm_mod(x, pre_mix_norm, sh1, sc1, lambda bi: bi).reshape(b * n, d)
    a_ctx = _norm_mod(ctx, pre_mix_norm, sh1, sc1, lambda bi: b).reshape(b * n_ctx, d)

    w_in_b = w_in[0].astype(BF16)
    cos, sin = _rope_tables(n)
    lay = _proj_layout(d)
    proj = _in_proj(a_lat, w_in_b, cos, sin, k_norm, sgu_norm, n, d)
    k_c, v_c = _ctx_kv(a_ctx, w_in_b, k_norm, d)

    attn = _attention(proj.reshape(b, n, lay["end"]), k_c.reshape(b, n_ctx, KV_WIDTH),
                      v_c.reshape(b, n_ctx, KV_WIDTH), q_norm, cos, sin, lay)
    merged = _merge(attn.reshape(b * n, ATTN_WIDTH), proj, sgu_w[0], sgu_b[0].T,
                    w_br_attn[0].astype(BF16), w_br_sgu[0].astype(BF16), n, d, lay)
    h1, b_lat = _out_proj(merged, w_out[0].astype(BF16), x.reshape(b * n, d), post_mix_norm, g1,
                          pre_ffn_norm, sh2, sc2, n, d)
    h1 = h1.reshape(b, n, d)

    tile = min(n, MOE_TOKEN_TILE)
    aff = _router(b_lat.reshape(b, n, d), w_router[0].T)
    slot, gate, idx, lo = _select(aff, cap, tile)
    xin = _gather(idx.reshape(b * n_e, 1, cap), b_lat, n)
    hid = _expert_up(xin, w_gate[0], w_up[0])
    y = _expert_down(hid, w_down[0], gate)
    lo_flat = lo[:, :, :n // tile + 1].reshape(-1)
    return _combine(lo_flat, slot, y, h1, post_ffn_norm, g2, tile)
```
